```python
import math
import jax, jax.numpy as jnp
from jax import lax
import numpy as np

D_MODEL = 2048
BATCH = 1
SEQ = 16384
DEPTH = 1

HEAD_DIM = 128
N_Q_HEADS = 16
N_KV_HEADS = 4
Q_PER_KV = N_Q_HEADS // N_KV_HEADS
ATTN_WIDTH = N_Q_HEADS * HEAD_DIM
KV_WIDTH = N_KV_HEADS * HEAD_DIM
WINDOW = 128
BLOCK = 128
POOL_WINDOWS = (2, 4, 8, 16)
N_POOL_GROUPS = len(POOL_WINDOWS)
POOL_GROUP_DIM = 256
POOL_WIDTH = N_POOL_GROUPS * POOL_GROUP_DIM
D_FF = 5632
ALPHA = (2.0 * DEPTH) ** 0.25
BETA = (8.0 * DEPTH) ** -0.25
LN_EPS = 1e-5
NEG_INF = -1e30
IN_WIDTH = ATTN_WIDTH + 2 * KV_WIDTH + POOL_WIDTH + 2 * D_MODEL

kernel_name = "hybrid_swa_pool_macaron_deepnorm"


def layer_norm(x, g, b):
    xf = x.astype(jnp.float32)
    mu = jnp.mean(xf, axis=-1, keepdims=True)
    xc = xf - mu
    var = jnp.mean(xc * xc, axis=-1, keepdims=True)
    y = xc * lax.rsqrt(var + LN_EPS)
    return (y * g.astype(jnp.float32) + b.astype(jnp.float32)).astype(x.dtype)


def swiglu(x, w_gate, w_up, w_down):
    return (jax.nn.silu(x @ w_gate) * (x @ w_up)) @ w_down


def alibi_slopes(n_heads):
    return jnp.exp2(-8.0 * jnp.arange(1, n_heads + 1, dtype=jnp.float32) / n_heads)


def banded_gqa_attention(q, k, v, sink):
    B, S = q.shape[0], q.shape[1]
    nb = S // BLOCK
    qb = q.reshape(B, nb, BLOCK, N_KV_HEADS, Q_PER_KV, HEAD_DIM)
    pad = ((0, 0), (BLOCK, BLOCK), (0, 0), (0, 0))
    kp = jnp.pad(k, pad).reshape(B, nb + 2, BLOCK, N_KV_HEADS, HEAD_DIM)
    vp = jnp.pad(v, pad).reshape(B, nb + 2, BLOCK, N_KV_HEADS, HEAD_DIM)
    kb = jnp.concatenate([kp[:, :-2], kp[:, 1:-1], kp[:, 2:]], axis=2)
    vb = jnp.concatenate([vp[:, :-2], vp[:, 1:-1], vp[:, 2:]], axis=2)
    scale = 1.0 / math.sqrt(HEAD_DIM)
    scores = jnp.einsum('bnqhgd,bnkhd->bnhgqk', qb, kb,
                        preferred_element_type=jnp.float32) * scale
    a = jnp.arange(BLOCK)[:, None]
    c = jnp.arange(3 * BLOCK)[None, :]
    dist = jnp.abs(a + BLOCK - c)
    slopes = alibi_slopes(N_Q_HEADS).reshape(N_KV_HEADS, Q_PER_KV)
    bias = -slopes[:, :, None, None] * dist.astype(jnp.float32)[None, None]
    k_pos = (jnp.arange(nb)[:, None] - 1) * BLOCK + jnp.arange(3 * BLOCK)[None, :]
    in_range = (k_pos >= 0) & (k_pos < S)
    valid = in_range[:, None, :] & (dist <= WINDOW)[None]
    scores = jnp.where(valid[None, :, None, None], scores + bias[None, None], NEG_INF)
    sink_l = sink.astype(jnp.float32).reshape(N_KV_HEADS, Q_PER_KV)[None, None, :, :, None, None]
    m = jnp.maximum(jnp.max(scores, axis=-1, keepdims=True), sink_l)
    p = jnp.exp(scores - m)
    denom = jnp.sum(p, axis=-1, keepdims=True) + jnp.exp(sink_l - m)
    probs = (p / denom).astype(v.dtype)
    o = jnp.einsum('bnhgqk,bnkhd->bnqhgd', probs, vb)
    return o.reshape(B, S, ATTN_WIDTH)


def multiscale_pool(p, w_groups, scale):
    B, S, _ = p.shape
    pf = p.astype(jnp.float32)
    cs = jnp.concatenate([jnp.zeros((B, 1, POOL_WIDTH), jnp.float32), jnp.cumsum(pf, axis=1)], axis=1)
    t = jnp.arange(S)
    outs = []
    for g, w in enumerate(POOL_WINDOWS):
        lo = jnp.clip(t - w // 2, 0, S)
        hi = jnp.clip(t + w - w // 2, 0, S)
        csg = cs[..., g * POOL_GROUP_DIM:(g + 1) * POOL_GROUP_DIM]
        win_sum = jnp.take(csg, hi, axis=1) - jnp.take(csg, lo, axis=1)
        mean = win_sum / (hi - lo).astype(jnp.float32)[None, :, None]
        outs.append(mean - pf[..., g * POOL_GROUP_DIM:(g + 1) * POOL_GROUP_DIM])
    d = jnp.stack(outs, axis=2).astype(p.dtype)
    mixed = jnp.einsum('bsgc,gcd->bsgd', d, w_groups).reshape(B, S, POOL_WIDTH)
    return mixed * scale


def hybrid_mixer(x, w_in, attn_sink, pool_w_groups, pool_scale, w_proj_attn, w_proj_pool, w_out):
    B, S, _ = x.shape
    h = x @ w_in
    i0 = ATTN_WIDTH
    i1 = i0 + KV_WIDTH
    i2 = i1 + KV_WIDTH
    i3 = i2 + POOL_WIDTH
    i4 = i3 + D_MODEL
    q = h[..., :i0].reshape(B, S, N_Q_HEADS, HEAD_DIM)
    k = h[..., i0:i1].reshape(B, S, N_KV_HEADS, HEAD_DIM)
    v = h[..., i1:i2].reshape(B, S, N_KV_HEADS, HEAD_DIM)
    pb = h[..., i2:i3]
    gate_a = h[..., i3:i4]
    gate_b = h[..., i4:]
    y_a = banded_gqa_attention(q, k, v, attn_sink) @ w_proj_attn
    y_b = multiscale_pool(pb, pool_w_groups, pool_scale) @ w_proj_pool
    merged = jax.nn.sigmoid(gate_a) * y_a + jax.nn.sigmoid(gate_b) * y_b
    return merged @ w_out


def setup_inputs(seed: int = 0) -> dict:
    key = jax.random.key(seed)
    ks = jax.random.split(key, 24)
    f32 = jnp.float32

    def nrm(k, shape, s):
        return jax.random.normal(k, shape, f32) * s

    L = DEPTH
    D = D_MODEL
    return {
        'x': jax.random.normal(ks[0], (BATCH, SEQ, D), f32),
        'ffn1_w_gate': nrm(ks[1], (L, D, D_FF), D ** -0.5),
        'ffn1_w_up': nrm(ks[2], (L, D, D_FF), D ** -0.5),
        'ffn1_w_down': nrm(ks[3], (L, D_FF, D), BETA * D_FF ** -0.5),
        'ln1_g': 1.0 + nrm(ks[4], (L, D), 0.05),
        'ln1_b': nrm(ks[5], (L, D), 0.02),
        'w_in': nrm(ks[6], (L, D, IN_WIDTH), D ** -0.5),
        'attn_sink': nrm(ks[7], (L, N_Q_HEADS), 0.5),
        'pool_w_groups': nrm(ks[8], (L, N_POOL_GROUPS, POOL_GROUP_DIM, POOL_GROUP_DIM), POOL_GROUP_DIM ** -0.5),
        'pool_scale': 1.0 + nrm(ks[9], (L, POOL_WIDTH), 0.1),
        'w_proj_attn': nrm(ks[10], (L, ATTN_WIDTH, D), ATTN_WIDTH ** -0.5),
        'w_proj_pool': nrm(ks[11], (L, POOL_WIDTH, D), POOL_WIDTH ** -0.5),
        'w_out': nrm(ks[12], (L, D, D), BETA * D ** -0.5),
        'ln2_g': 1.0 + nrm(ks[13], (L, D), 0.05),
        'ln2_b': nrm(ks[14], (L, D), 0.02),
        'ffn2_w_gate': nrm(ks[15], (L, D, D_FF), D ** -0.5),
        'ffn2_w_up': nrm(ks[16], (L, D, D_FF), D ** -0.5),
        'ffn2_w_down': nrm(ks[17], (L, D_FF, D), BETA * D_FF ** -0.5),
        'ln3_g': 1.0 + nrm(ks[18], (L, D), 0.05),
        'ln3_b': nrm(ks[19], (L, D), 0.02),
    }


def reference(x, ffn1_w_gate, ffn1_w_up, ffn1_w_down, ln1_g, ln1_b,
              w_in, attn_sink, pool_w_groups, pool_scale, w_proj_attn, w_proj_pool, w_out,
              ln2_g, ln2_b, ffn2_w_gate, ffn2_w_up, ffn2_w_down, ln3_g, ln3_b):
    for l in range(DEPTH):
        x = layer_norm(ALPHA * x + 0.5 * swiglu(x, ffn1_w_gate[l], ffn1_w_up[l], ffn1_w_down[l]),
                       ln1_g[l], ln1_b[l])
        x = layer_norm(ALPHA * x + hybrid_mixer(x, w_in[l], attn_sink[l], pool_w_groups[l], pool_scale[l],
                                                w_proj_attn[l], w_proj_pool[l], w_out[l]),
                       ln2_g[l], ln2_b[l])
        x = layer_norm(ALPHA * x + 0.5 * swiglu(x, ffn2_w_gate[l], ffn2_w_up[l], ffn2_w_down[l]),
                       ln3_g[l], ln3_b[l])
    return x
```

```python
import functools
import math

import numpy as np
import jax
import jax.numpy as jnp
from jax import lax
from jax.experimental import pallas as pl
from jax.experimental.pallas import tpu as pltpu

D_MODEL = 2048
SEQ = 16384
HEAD_DIM = 128
N_Q_HEADS = 16
N_KV_HEADS = 4
Q_PER_KV = N_Q_HEADS // N_KV_HEADS
ATTN_WIDTH = N_Q_HEADS * HEAD_DIM
KV_WIDTH = N_KV_HEADS * HEAD_DIM
WINDOW = 128
BLOCK = 128
POOL_WINDOWS = (2, 4, 8, 16)
N_POOL_GROUPS = len(POOL_WINDOWS)
POOL_GROUP_DIM = 256
POOL_WIDTH = N_POOL_GROUPS * POOL_GROUP_DIM
D_FF = 5632
ALPHA = 2.0 ** 0.25
LN_EPS = 1e-5
NEG_INF = -1e30
IN_WIDTH = ATTN_WIDTH + 2 * KV_WIDTH + POOL_WIDTH + 2 * D_MODEL

Q_COL = 0
K_COL = ATTN_WIDTH // HEAD_DIM
V_COL = (ATTN_WIDTH + KV_WIDTH) // HEAD_DIM
POOL_OFF = ATTN_WIDTH + 2 * KV_WIDTH
GATE_A_OFF = POOL_OFF + POOL_WIDTH
GATE_B_OFF = GATE_A_OFF + D_MODEL

V7X_VMEM_LIMIT_BYTES = 56 * 1024 * 1024
POOL_HALO = 16

FFN_TM = 512
FFN_TF = 512
PROJ_TM = 1024
PROJ_TN = 1024
ATTN_TQ = 1024
MIX_TM = 256

F32 = jnp.float32
BF16 = jnp.bfloat16


def _layer_norm(y, g, b):
    mu = jnp.mean(y, axis=-1, keepdims=True)
    yc = y - mu
    var = jnp.mean(yc * yc, axis=-1, keepdims=True)
    return yc * lax.rsqrt(var + LN_EPS) * g + b


def _sigmoid(x):
    return 1.0 / (1.0 + jnp.exp(-x))


def _ffn_ln_kernel(x_ref, wg_ref, wu_ref, wd_ref, g_ref, b_ref, *rest):
    o_ref, xb_ref, acc_ref = rest[0], rest[-2], rest[-1]
    ob_ref = rest[1] if len(rest) == 4 else None
    j = pl.program_id(1)

    @pl.when(j == 0)
    def _():
        xb_ref[...] = x_ref[...].astype(BF16)
        acc_ref[...] = jnp.zeros_like(acc_ref)

    xb = xb_ref[...]
    gate = jnp.dot(xb, wg_ref[...], preferred_element_type=F32)
    up = jnp.dot(xb, wu_ref[...], preferred_element_type=F32)
    act = (gate * _sigmoid(gate) * up).astype(BF16)
    acc_ref[...] += jnp.dot(act, wd_ref[...], preferred_element_type=F32)

    @pl.when(j == pl.num_programs(1) - 1)
    def _():
        y = _layer_norm(ALPHA * x_ref[...] + 0.5 * acc_ref[...], g_ref[...], b_ref[...])
        o_ref[...] = y
        if ob_ref is not None:
            ob_ref[...] = y.astype(BF16)


def _ffn_ln(x, wg, wu, wd, g, b, name, with_bf16_copy):
    s, d = x.shape
    f = wg.shape[1]
    tm, tf = FFN_TM, FFN_TF
    n_out = 2 if with_bf16_copy else 1
    return pl.pallas_call(
        _ffn_ln_kernel,
        grid=(s // tm, f // tf),
        in_specs=[
            pl.BlockSpec((tm, d), lambda i, j: (i, 0)),
            pl.BlockSpec((d, tf), lambda i, j: (0, j)),
            pl.BlockSpec((d, tf), lambda i, j: (0, j)),
            pl.BlockSpec((tf, d), lambda i, j: (j, 0)),
            pl.BlockSpec((1, d), lambda i, j: (0, 0)),
            pl.BlockSpec((1, d), lambda i, j: (0, 0)),
        ],
        out_specs=[pl.BlockSpec((tm, d), lambda i, j: (i, 0))] * n_out,
        out_shape=[jax.ShapeDtypeStruct((s, d), F32), jax.ShapeDtypeStruct((s, d), BF16)][:n_out],
        scratch_shapes=[pltpu.VMEM((tm, d), BF16), pltpu.VMEM((tm, d), F32)],
        compiler_params=pltpu.CompilerParams(
            dimension_semantics=("parallel", "arbitrary"),
            vmem_limit_bytes=V7X_VMEM_LIMIT_BYTES),
        name=name,
    )(x, wg, wu, wd, g, b)


def _proj_kernel(x_ref, w_ref, o_ref):
    o_ref[...] = jnp.dot(x_ref[...], w_ref[...], preferred_element_type=F32).astype(o_ref.dtype)


def _in_proj(xb, w):
    s, d = xb.shape
    n = w.shape[1]
    tm, tn = PROJ_TM, PROJ_TN
    return pl.pallas_call(
        _proj_kernel,
        grid=(s // tm, n // tn),
        in_specs=[
            pl.BlockSpec((tm, d), lambda i, j: (i, 0)),
            pl.BlockSpec((d, tn), lambda i, j: (0, j)),
        ],
        out_specs=pl.BlockSpec((tm, tn), lambda i, j: (i, j)),
        out_shape=jax.ShapeDtypeStruct((s, n), BF16),
        compiler_params=pltpu.CompilerParams(
            dimension_semantics=("parallel", "parallel"),
            vmem_limit_bytes=V7X_VMEM_LIMIT_BYTES),
        name="mixer_in_proj",
    )(xb, w)


def _attn_kernel(q_ref, kp_ref, kc_ref, kn_ref, vp_ref, vc_ref, vn_ref, bias_ref, sink_ref,
                 o_ref, kcat_ref, vcat_ref):
    i = pl.program_id(1)
    tq = q_ref.shape[0]
    blocks_per_tile = tq // BLOCK
    n_blocks = pl.num_programs(1) * blocks_per_tile

    kcat_ref[0:BLOCK, :] = kp_ref[...]
    kcat_ref[BLOCK:BLOCK + tq, :] = kc_ref[...]
    kcat_ref[BLOCK + tq:, :] = kn_ref[...]
    vcat_ref[0:BLOCK, :] = vp_ref[...]
    vcat_ref[BLOCK:BLOCK + tq, :] = vc_ref[...]
    vcat_ref[BLOCK + tq:, :] = vn_ref[...]

    sink = sink_ref[0]
    scale = 1.0 / math.sqrt(HEAD_DIM)

    def body(b, carry):
        row = pl.multiple_of(b * BLOCK, BLOCK)
        gb = i * blocks_per_tile + b
        var = jnp.where(gb == 0, 0, jnp.where(gb == n_blocks - 1, 2, 1))
        qb = q_ref[pl.ds(row, BLOCK), :]
        q4 = jnp.concatenate(
            [qb[:, g * HEAD_DIM:(g + 1) * HEAD_DIM] for g in range(Q_PER_KV)], axis=0)
        kw = kcat_ref[pl.ds(row, 3 * BLOCK), :]
        vw = vcat_ref[pl.ds(row, 3 * BLOCK), :]
        s = lax.dot_general(q4, kw, (((1,), (1,)), ((), ())), preferred_element_type=F32)
        s = s * scale + bias_ref[0, var]
        m = jnp.maximum(jnp.max(s, axis=-1, keepdims=True), sink)
        p = jnp.exp(s - m)
        denom = jnp.sum(p, axis=-1, keepdims=True) + jnp.exp(sink - m)
        o = jnp.dot(p.astype(BF16), vw, preferred_element_type=F32) * (1.0 / denom)
        for g in range(Q_PER_KV):
            o_ref[pl.ds(row, BLOCK), g * HEAD_DIM:(g + 1) * HEAD_DIM] = (
                o[g * BLOCK:(g + 1) * BLOCK].astype(o_ref.dtype))
        return carry

    lax.fori_loop(0, blocks_per_tile, body, 0)


def _attention(h, bias, sink_rows):
    s = h.shape[0]
    tq = ATTN_TQ
    r = tq // BLOCK
    nb = s // BLOCK
    assert nb >= 2
    gw = Q_PER_KV * HEAD_DIM

    def prev_map(col):
        return lambda hk, i: (jnp.maximum(i * r - 1, 0), col + hk)

    def cur_map(col):
        return lambda hk, i: (i, col + hk)

    def next_map(col):
        return lambda hk, i: (jnp.minimum((i + 1) * r, nb - 1), col + hk)

    return pl.pallas_call(
        _attn_kernel,
        grid=(N_KV_HEADS, s // tq),
        in_specs=[
            pl.BlockSpec((tq, gw), lambda hk, i: (i, hk)),
            pl.BlockSpec((BLOCK, HEAD_DIM), prev_map(K_COL)),
            pl.BlockSpec((tq, HEAD_DIM), cur_map(K_COL)),
            pl.BlockSpec((BLOCK, HEAD_DIM), next_map(K_COL)),
            pl.BlockSpec((BLOCK, HEAD_DIM), prev_map(V_COL)),
            pl.BlockSpec((tq, HEAD_DIM), cur_map(V_COL)),
            pl.BlockSpec((BLOCK, HEAD_DIM), next_map(V_COL)),
            pl.BlockSpec((1, 3, Q_PER_KV * BLOCK, 3 * BLOCK), lambda hk, i: (hk, 0, 0, 0)),
            pl.BlockSpec((1, Q_PER_KV * BLOCK, 1), lambda hk, i: (hk, 0, 0)),
        ],
        out_specs=pl.BlockSpec((tq, gw), lambda hk, i: (i, hk)),
        out_shape=jax.ShapeDtypeStruct((s, ATTN_WIDTH), BF16),
        scratch_shapes=[
            pltpu.VMEM((tq + 2 * BLOCK, HEAD_DIM), BF16),
            pltpu.VMEM((tq + 2 * BLOCK, HEAD_DIM), BF16),
        ],
        compiler_params=pltpu.CompilerParams(
            dimension_semantics=("parallel", "parallel"),
            vmem_limit_bytes=V7X_VMEM_LIMIT_BYTES),
        name="banded_attention",
    )(h, h, h, h, h, h, h, bias, sink_rows)


def _attn_bias_table():
    a = jnp.arange(BLOCK)[:, None]
    c = jnp.arange(3 * BLOCK)[None, :]
    dist = jnp.abs(a + BLOCK - c)
    slopes = jnp.exp2(-8.0 * jnp.arange(1, N_Q_HEADS + 1, dtype=F32) / N_Q_HEADS)
    bias = -slopes[:, None, None] * dist.astype(F32)[None]
    in_window = dist <= WINDOW
    variants = jnp.stack([in_window & (c >= BLOCK), in_window, in_window & (c < 2 * BLOCK)])
    table = jnp.where(variants[None], bias[:, None], NEG_INF)
    table = table.reshape(N_KV_HEADS, Q_PER_KV, 3, BLOCK, 3 * BLOCK)
    table = table.transpose(0, 2, 1, 3, 4)
    return table.reshape(N_KV_HEADS, 3, Q_PER_KV * BLOCK, 3 * BLOCK)


def _pool_band_matrices(tm):
    t = np.arange(tm)[:, None]
    col = np.arange(tm + 2 * POOL_HALO)[None, :]
    pos = np.where(col < tm + POOL_HALO, col, col - tm - 2 * POOL_HALO)
    mats = []
    for w in POOL_WINDOWS:
        lo = t - w // 2
        hi = t + w - w // 2
        mats.append(((pos >= lo) & (pos < hi)).astype(np.float32))
    return np.stack(mats)


def _mixer_out_kernel(o_ref, pp_ref, pc_ref, pn_ref, ga_ref, gb_ref, x_ref, band_ref, wgrp_ref,
                      pscale_ref, wpa_ref, wpp_ref, wout_ref, g_ref, b_ref,
                      out_ref, pcat_ref):
    i = pl.program_id(0)
    tm = pc_ref.shape[0]
    seq = pl.num_programs(0) * tm

    zeros_halo = jnp.zeros((POOL_HALO, POOL_WIDTH), BF16)
    pcat_ref[0:tm, :] = pc_ref[...]
    pcat_ref[tm:tm + POOL_HALO, :] = jnp.where(i == pl.num_programs(0) - 1, zeros_halo, pn_ref[...])
    pcat_ref[tm + POOL_HALO:, :] = jnp.where(i == 0, zeros_halo, pp_ref[...])

    t = i * tm + lax.broadcasted_iota(jnp.int32, (tm, 1), 0)
    mixed = []
    for g, w in enumerate(POOL_WINDOWS):
        cols = slice(g * POOL_GROUP_DIM, (g + 1) * POOL_GROUP_DIM)
        pg = pcat_ref[:, cols]
        win_sum = jnp.dot(band_ref[g], pg, preferred_element_type=F32)
        lo = jnp.clip(t - w // 2, 0, seq)
        hi = jnp.clip(t + w - w // 2, 0, seq)
        mean = win_sum / (hi - lo).astype(F32)
        d = (mean - pc_ref[:, cols].astype(F32)).astype(BF16)
        mg = jnp.dot(d, wgrp_ref[g], preferred_element_type=F32) * pscale_ref[:, cols]
        mixed.append(mg.astype(BF16))
    mixed = jnp.concatenate(mixed, axis=-1)

    y_a = jnp.dot(o_ref[...], wpa_ref[...], preferred_element_type=F32)
    y_b = jnp.dot(mixed, wpp_ref[...], preferred_element_type=F32)
    merged = _sigmoid(ga_ref[...].astype(F32)) * y_a + _sigmoid(gb_ref[...].astype(F32)) * y_b
    z = jnp.dot(merged.astype(BF16), wout_ref[...], preferred_element_type=F32)
    out_ref[...] = _layer_norm(ALPHA * x_ref[...] + z, g_ref[...], b_ref[...])


def _mixer_out(o, h, x, band, wgrp, pscale, wpa, wpp, wout, g, b):
    s, d = x.shape
    tm = MIX_TM
    hr = tm // POOL_HALO
    n_halo = s // POOL_HALO
    pool_cb = POOL_OFF // POOL_WIDTH
    ga_cb = GATE_A_OFF // D_MODEL
    gb_cb = GATE_B_OFF // D_MODEL
    assert POOL_OFF % POOL_WIDTH == 0 and GATE_A_OFF % D_MODEL == 0 and GATE_B_OFF % D_MODEL == 0

    def const(shape):
        return pl.BlockSpec(shape, lambda i: (0,) * len(shape), pipeline_mode=pl.Buffered(1))

    return pl.pallas_call(
        _mixer_out_kernel,
        grid=(s // tm,),
        in_specs=[
            pl.BlockSpec((tm, ATTN_WIDTH), lambda i: (i, 0)),
            pl.BlockSpec((POOL_HALO, POOL_WIDTH), lambda i: (jnp.maximum(i * hr - 1, 0), pool_cb)),
            pl.BlockSpec((tm, POOL_WIDTH), lambda i: (i, pool_cb)),
            pl.BlockSpec((POOL_HALO, POOL_WIDTH),
                         lambda i: (jnp.minimum((i + 1) * hr, n_halo - 1), pool_cb)),
            pl.BlockSpec((tm, D_MODEL), lambda i: (i, ga_cb)),
            pl.BlockSpec((tm, D_MODEL), lambda i: (i, gb_cb)),
            pl.BlockSpec((tm, d), lambda i: (i, 0)),
            const(band.shape),
            const(wgrp.shape),
            const(pscale.shape),
            const(wpa.shape),
            const(wpp.shape),
            const(wout.shape),
            const(g.shape),
            const(b.shape),
        ],
        out_specs=pl.BlockSpec((tm, d), lambda i: (i, 0)),
        out_shape=jax.ShapeDtypeStruct((s, d), F32),
        scratch_shapes=[pltpu.VMEM((tm + 2 * POOL_HALO, POOL_WIDTH), BF16)],
        compiler_params=pltpu.CompilerParams(
            dimension_semantics=("parallel",),
            vmem_limit_bytes=V7X_VMEM_LIMIT_BYTES),
        name="mixer_out_ln",
    )(o, h, h, h, h, h, x, band, wgrp, pscale, wpa, wpp, wout, g, b)


def kernel(x, ffn1_w_gate, ffn1_w_up, ffn1_w_down, ln1_g, ln1_b, w_in, attn_sink, pool_w_groups,
           pool_scale, w_proj_attn, w_proj_pool, w_out, ln2_g, ln2_b, ffn2_w_gate, ffn2_w_up,
           ffn2_w_down, ln3_g, ln3_b):
    batch, seq, d = x.shape
    assert (batch, seq, d) == (1, SEQ, D_MODEL)
    x = x.reshape(seq, d)

    bias = _attn_bias_table()
    band = jnp.asarray(_pool_band_matrices(MIX_TM), BF16)
    depth = ffn1_w_gate.shape[0]
    for l in range(depth):
        x, xb = _ffn_ln(x, ffn1_w_gate[l].astype(BF16), ffn1_w_up[l].astype(BF16),
                        ffn1_w_down[l].astype(BF16), ln1_g[l][None], ln1_b[l][None],
                        "ffn1_ln1", True)
        h = _in_proj(xb, w_in[l].astype(BF16))
        sink_rows = jnp.repeat(attn_sink[l].astype(F32), BLOCK).reshape(
            N_KV_HEADS, Q_PER_KV * BLOCK, 1)
        o = _attention(h, bias, sink_rows)
        x = _mixer_out(o, h, x, band, pool_w_groups[l].astype(BF16), pool_scale[l][None],
                       w_proj_attn[l].astype(BF16), w_proj_pool[l].astype(BF16),
                       w_out[l].astype(BF16), ln2_g[l][None], ln2_b[l][None])
        (x,) = _ffn_ln(x, ffn2_w_gate[l].astype(BF16), ffn2_w_up[l].astype(BF16),
                       ffn2_w_down[l].astype(BF16), ln3_g[l][None], ln3_b[l][None],
                       "ffn2_ln3", False)
    return x.reshape(batch, seq, d)
```

```python
import math

import numpy as np
import jax
import jax.numpy as jnp
from jax import lax
from jax.experimental import pallas as pl
from jax.experimental.pallas import tpu as pltpu

D_MODEL = 2048
SEQ = 16384
HEAD_DIM = 128
N_Q_HEADS = 16
N_KV_HEADS = 4
Q_PER_KV = N_Q_HEADS // N_KV_HEADS
ATTN_WIDTH = N_Q_HEADS * HEAD_DIM
KV_WIDTH = N_KV_HEADS * HEAD_DIM
WINDOW = 128
BLOCK = 128
POOL_WINDOWS = (2, 4, 8, 16)
N_POOL_GROUPS = len(POOL_WINDOWS)
POOL_GROUP_DIM = 256
POOL_WIDTH = N_POOL_GROUPS * POOL_GROUP_DIM
D_FF = 5632
ALPHA = 2.0 ** 0.25
LN_EPS = 1e-5
NEG_INF = -1e30
LOG2E = math.log2(math.e)
IN_WIDTH = ATTN_WIDTH + 2 * KV_WIDTH + POOL_WIDTH + 2 * D_MODEL

Q_COL = 0
K_COL = ATTN_WIDTH // HEAD_DIM
V_COL = (ATTN_WIDTH + KV_WIDTH) // HEAD_DIM
POOL_OFF = ATTN_WIDTH + 2 * KV_WIDTH
GATE_A_OFF = POOL_OFF + POOL_WIDTH
GATE_B_OFF = GATE_A_OFF + D_MODEL

V7X_VMEM_LIMIT_BYTES = 56 * 1024 * 1024
POOL_HALO = 16

FFN_TM = 512
FFN_TF = 512
PROJ_TM = 1024
PROJ_TN = 1024
ATTN_TQ = 1024
MIX_TM = 256

F32 = jnp.float32
BF16 = jnp.bfloat16


def _layer_norm(y, g, b):
    mu = jnp.mean(y, axis=-1, keepdims=True)
    yc = y - mu
    var = jnp.mean(yc * yc, axis=-1, keepdims=True)
    return yc * lax.rsqrt(var + LN_EPS) * g + b


def _sigmoid(x):
    return 1.0 / (1.0 + jnp.exp(-x))


def _ffn_ln_kernel(x_ref, wg_ref, wu_ref, wd_ref, g_ref, b_ref, *rest):
    o_ref, xb_ref, acc_ref = rest[0], rest[-2], rest[-1]
    ob_ref = rest[1] if len(rest) == 4 else None
    j = pl.program_id(1)

    @pl.when(j == 0)
    def _():
        xb_ref[...] = x_ref[...].astype(BF16)
        acc_ref[...] = jnp.zeros_like(acc_ref)

    xb = xb_ref[...]
    gate = jnp.dot(xb, wg_ref[...], preferred_element_type=F32)
    up = jnp.dot(xb, wu_ref[...], preferred_element_type=F32)
    act = (gate * _sigmoid(gate) * up).astype(BF16)
    acc_ref[...] += jnp.dot(act, wd_ref[...], preferred_element_type=F32)

    @pl.when(j == pl.num_programs(1) - 1)
    def _():
        y = _layer_norm(ALPHA * x_ref[...] + 0.5 * acc_ref[...], g_ref[...], b_ref[...])
        o_ref[...] = y
        if ob_ref is not None:
            ob_ref[...] = y.astype(BF16)


def _ffn_ln(x, wg, wu, wd, g, b, name, with_bf16_copy):
    s, d = x.shape
    f = wg.shape[1]
    tm, tf = FFN_TM, FFN_TF
    n_out = 2 if with_bf16_copy else 1
    return pl.pallas_call(
        _ffn_ln_kernel,
        grid=(s // tm, f // tf),
        in_specs=[
            pl.BlockSpec((tm, d), lambda i, j: (i, 0)),
            pl.BlockSpec((d, tf), lambda i, j: (0, j)),
            pl.BlockSpec((d, tf), lambda i, j: (0, j)),
            pl.BlockSpec((tf, d), lambda i, j: (j, 0)),
            pl.BlockSpec((1, d), lambda i, j: (0, 0)),
            pl.BlockSpec((1, d), lambda i, j: (0, 0)),
        ],
        out_specs=[pl.BlockSpec((tm, d), lambda i, j: (i, 0))] * n_out,
        out_shape=[jax.ShapeDtypeStruct((s, d), F32), jax.ShapeDtypeStruct((s, d), BF16)][:n_out],
        scratch_shapes=[pltpu.VMEM((tm, d), BF16), pltpu.VMEM((tm, d), F32)],
        compiler_params=pltpu.CompilerParams(
            dimension_semantics=("parallel", "arbitrary"),
            vmem_limit_bytes=V7X_VMEM_LIMIT_BYTES),
        name=name,
    )(x, wg, wu, wd, g, b)


def _proj_kernel(x_ref, w_ref, o_ref):
    o_ref[...] = jnp.dot(x_ref[...], w_ref[...], preferred_element_type=F32).astype(o_ref.dtype)


def _in_proj(xb, w):
    s, d = xb.shape
    n = w.shape[1]
    tm, tn = PROJ_TM, PROJ_TN
    return pl.pallas_call(
        _proj_kernel,
        grid=(s // tm, n // tn),
        in_specs=[
            pl.BlockSpec((tm, d), lambda i, j: (i, 0)),
            pl.BlockSpec((d, tn), lambda i, j: (0, j)),
        ],
        out_specs=pl.BlockSpec((tm, tn), lambda i, j: (i, j)),
        out_shape=jax.ShapeDtypeStruct((s, n), BF16),
        compiler_params=pltpu.CompilerParams(
            dimension_semantics=("parallel", "parallel"),
            vmem_limit_bytes=V7X_VMEM_LIMIT_BYTES),
        name="mixer_in_proj",
    )(xb, w)


def _attn_kernel(sink_ref, q_ref, kp_ref, kc_ref, kn_ref, vp_ref, vc_ref, vn_ref, bias_ref,
                 o_ref, kcat_ref, vcat_ref):
    hk = pl.program_id(0)
    i = pl.program_id(1)
    tq = q_ref.shape[0]
    blocks_per_tile = tq // BLOCK
    n_blocks = pl.num_programs(1) * blocks_per_tile

    kcat_ref[0:BLOCK, :] = kp_ref[...]
    kcat_ref[BLOCK:BLOCK + tq, :] = kc_ref[...]
    kcat_ref[BLOCK + tq:, :] = kn_ref[...]
    vcat_ref[0:BLOCK, 0:HEAD_DIM] = vp_ref[...]
    vcat_ref[BLOCK:BLOCK + tq, 0:HEAD_DIM] = vc_ref[...]
    vcat_ref[BLOCK + tq:, 0:HEAD_DIM] = vn_ref[...]
    vcat_ref[:, HEAD_DIM:] = jnp.ones((tq + 2 * BLOCK, HEAD_DIM), BF16)

    qk_scale = LOG2E / math.sqrt(HEAD_DIM)

    def body(b, carry):
        row = pl.multiple_of(b * BLOCK, BLOCK)
        gb = i * blocks_per_tile + b
        var = jnp.where(gb == 0, 0, jnp.where(gb == n_blocks - 1, 2, 1))
        kw = kcat_ref[pl.ds(row, 3 * BLOCK), :]
        vw = vcat_ref[pl.ds(row, 3 * BLOCK), :]
        for g in range(Q_PER_KV):
            cols = slice(g * HEAD_DIM, (g + 1) * HEAD_DIM)
            sink = sink_ref[hk * Q_PER_KV + g]
            q = q_ref[pl.ds(row, BLOCK), cols]
            s = lax.dot_general(q, kw, (((1,), (1,)), ((), ())), preferred_element_type=F32)
            s = s * qk_scale + bias_ref[0, var, g]
            m = jnp.maximum(jnp.max(s, axis=-1, keepdims=True), sink)
            m_b = jnp.broadcast_to(m, (BLOCK, HEAD_DIM))
            p = jnp.exp2(s - jnp.concatenate([m_b] * 3, axis=-1))
            ov = jnp.dot(p.astype(BF16), vw, preferred_element_type=F32)
            denom = ov[:, HEAD_DIM:] + jnp.exp2(sink - m_b)
            o_ref[pl.ds(row, BLOCK), cols] = (ov[:, :HEAD_DIM] / denom).astype(o_ref.dtype)
        return carry

    lax.fori_loop(0, blocks_per_tile, body, 0, unroll=2)


def _attention(h, bias, sink_log2):
    s = h.shape[0]
    tq = ATTN_TQ
    r = tq // BLOCK
    nb = s // BLOCK
    assert nb >= 2
    gw = Q_PER_KV * HEAD_DIM

    def prev_map(col):
        return lambda hk, i: (jnp.maximum(i * r - 1, 0), col + hk)

    def cur_map(col):
        return lambda hk, i: (i, col + hk)

    def next_map(col):
        return lambda hk, i: (jnp.minimum((i + 1) * r, nb - 1), col + hk)

    return pl.pallas_call(
        _attn_kernel,
        grid=(N_KV_HEADS, s // tq),
        in_specs=[
            pl.BlockSpec(memory_space=pltpu.SMEM),
            pl.BlockSpec((tq, gw), lambda hk, i: (i, hk)),
            pl.BlockSpec((BLOCK, HEAD_DIM), prev_map(K_COL)),
            pl.BlockSpec((tq, HEAD_DIM), cur_map(K_COL)),
            pl.BlockSpec((BLOCK, HEAD_DIM), next_map(K_COL)),
            pl.BlockSpec((BLOCK, HEAD_DIM), prev_map(V_COL)),
            pl.BlockSpec((tq, HEAD_DIM), cur_map(V_COL)),
            pl.BlockSpec((BLOCK, HEAD_DIM), next_map(V_COL)),
            pl.BlockSpec((1, 3, Q_PER_KV, BLOCK, 3 * BLOCK), lambda hk, i: (hk, 0, 0, 0, 0)),
        ],
        out_specs=pl.BlockSpec((tq, gw), lambda hk, i: (i, hk)),
        out_shape=jax.ShapeDtypeStruct((s, ATTN_WIDTH), BF16),
        scratch_shapes=[
            pltpu.VMEM((tq + 2 * BLOCK, HEAD_DIM), BF16),
            pltpu.VMEM((tq + 2 * BLOCK, 2 * HEAD_DIM), BF16),
        ],
        compiler_params=pltpu.CompilerParams(
            dimension_semantics=("parallel", "parallel"),
            vmem_limit_bytes=V7X_VMEM_LIMIT_BYTES),
        name="banded_attention",
    )(sink_log2, h, h, h, h, h, h, h, bias)


def _attn_bias_table():
    a = jnp.arange(BLOCK)[:, None]
    c = jnp.arange(3 * BLOCK)[None, :]
    dist = jnp.abs(a + BLOCK - c)
    slopes = jnp.exp2(-8.0 * jnp.arange(1, N_Q_HEADS + 1, dtype=F32) / N_Q_HEADS)
    bias = -slopes[:, None, None] * dist.astype(F32)[None] * LOG2E
    in_window = dist <= WINDOW
    variants = jnp.stack([in_window & (c >= BLOCK), in_window, in_window & (c < 2 * BLOCK)])
    table = jnp.where(variants[None], bias[:, None], NEG_INF)
    table = table.reshape(N_KV_HEADS, Q_PER_KV, 3, BLOCK, 3 * BLOCK)
    return table.transpose(0, 2, 1, 3, 4)


def _pool_band_matrices(tm):
    t = np.arange(tm)[:, None]
    col = np.arange(tm + 2 * POOL_HALO)[None, :]
    pos = np.where(col < tm + POOL_HALO, col, col - tm - 2 * POOL_HALO)
    mats = []
    for w in POOL_WINDOWS:
        lo = t - w // 2
        hi = t + w - w // 2
        mats.append(((pos >= lo) & (pos < hi)).astype(np.float32))
    return np.stack(mats)


def _mixer_out_kernel(o_ref, pp_ref, pc_ref, pn_ref, ga_ref, gb_ref, x_ref, band_ref, wgrp_ref,
                      pscale_ref, wpa_ref, wpp_ref, wout_ref, g_ref, b_ref,
                      out_ref, pcat_ref):
    i = pl.program_id(0)
    tm = pc_ref.shape[0]
    seq = pl.num_programs(0) * tm

    zeros_halo = jnp.zeros((POOL_HALO, POOL_WIDTH), BF16)
    pcat_ref[0:tm, :] = pc_ref[...]
    pcat_ref[tm:tm + POOL_HALO, :] = jnp.where(i == pl.num_programs(0) - 1, zeros_halo, pn_ref[...])
    pcat_ref[tm + POOL_HALO:, :] = jnp.where(i == 0, zeros_halo, pp_ref[...])

    t = i * tm + lax.broadcasted_iota(jnp.int32, (tm, 1), 0)
    mixed = []
    for g, w in enumerate(POOL_WINDOWS):
        cols = slice(g * POOL_GROUP_DIM, (g + 1) * POOL_GROUP_DIM)
        pg = pcat_ref[:, cols]
        win_sum = jnp.dot(band_ref[g], pg, preferred_element_type=F32)
        lo = jnp.clip(t - w // 2, 0, seq)
        hi = jnp.clip(t + w - w // 2, 0, seq)
        mean = win_sum / (hi - lo).astype(F32)
        d = (mean - pc_ref[:, cols].astype(F32)).astype(BF16)
        mg = jnp.dot(d, wgrp_ref[g], preferred_element_type=F32) * pscale_ref[:, cols]
        mixed.append(mg.astype(BF16))
    mixed = jnp.concatenate(mixed, axis=-1)

    y_a = jnp.dot(o_ref[...], wpa_ref[...], preferred_element_type=F32)
    y_b = jnp.dot(mixed, wpp_ref[...], preferred_element_type=F32)
    merged = _sigmoid(ga_ref[...].astype(F32)) * y_a + _sigmoid(gb_ref[...].astype(F32)) * y_b
    z = jnp.dot(merged.astype(BF16), wout_ref[...], preferred_element_type=F32)
    out_ref[...] = _layer_norm(ALPHA * x_ref[...] + z, g_ref[...], b_ref[...])


def _mixer_out(o, h, x, band, wgrp, pscale, wpa, wpp, wout, g, b):
    s, d = x.shape
    tm = MIX_TM
    hr = tm // POOL_HALO
    n_halo = s // POOL_HALO
    pool_cb = POOL_OFF // POOL_WIDTH
    ga_cb = GATE_A_OFF // D_MODEL
    gb_cb = GATE_B_OFF // D_MODEL
    assert POOL_OFF % POOL_WIDTH == 0 and GATE_A_OFF % D_MODEL == 0 and GATE_B_OFF % D_MODEL == 0

    def const(shape):
        return pl.BlockSpec(shape, lambda i: (0,) * len(shape), pipeline_mode=pl.Buffered(1))

    return pl.pallas_call(
        _mixer_out_kernel,
        grid=(s // tm,),
        in_specs=[
            pl.BlockSpec((tm, ATTN_WIDTH), lambda i: (i, 0)),
            pl.BlockSpec((POOL_HALO, POOL_WIDTH), lambda i: (jnp.maximum(i * hr - 1, 0), pool_cb)),
            pl.BlockSpec((tm, POOL_WIDTH), lambda i: (i, pool_cb)),
            pl.BlockSpec((POOL_HALO, POOL_WIDTH),
                         lambda i: (jnp.minimum((i + 1) * hr, n_halo - 1), pool_cb)),
            pl.BlockSpec((tm, D_MODEL), lambda i: (i, ga_cb)),
            pl.BlockSpec((tm, D_MODEL), lambda i: (i, gb_cb)),
            pl.BlockSpec((tm, d), lambda i: (i, 0)),
            const(band.shape),
            const(wgrp.shape),
            const(pscale.shape),
            const(wpa.shape),
            const(wpp.shape),
            const(wout.shape),
            const(g.shape),
            const(b.shape),
        ],
        out_specs=pl.BlockSpec((tm, d), lambda i: (i, 0)),
        out_shape=jax.ShapeDtypeStruct((s, d), F32),
        scratch_shapes=[pltpu.VMEM((tm + 2 * POOL_HALO, POOL_WIDTH), BF16)],
        compiler_params=pltpu.CompilerParams(
            dimension_semantics=("parallel",),
            vmem_limit_bytes=V7X_VMEM_LIMIT_BYTES),
        name="mixer_out_ln",
    )(o, h, h, h, h, h, x, band, wgrp, pscale, wpa, wpp, wout, g, b)


def kernel(x, ffn1_w_gate, ffn1_w_up, ffn1_w_down, ln1_g, ln1_b, w_in, attn_sink, pool_w_groups,
           pool_scale, w_proj_attn, w_proj_pool, w_out, ln2_g, ln2_b, ffn2_w_gate, ffn2_w_up,
           ffn2_w_down, ln3_g, ln3_b):
    batch, seq, d = x.shape
    assert (batch, seq, d) == (1, SEQ, D_MODEL)
    x = x.reshape(seq, d)

    bias = _attn_bias_table()
    band = jnp.asarray(_pool_band_matrices(MIX_TM), BF16)
    depth = ffn1_w_gate.shape[0]
    for l in range(depth):
        x, xb = _ffn_ln(x, ffn1_w_gate[l].astype(BF16), ffn1_w_up[l].astype(BF16),
                        ffn1_w_down[l].astype(BF16), ln1_g[l][None], ln1_b[l][None],
                        "ffn1_ln1", True)
        h = _in_proj(xb, w_in[l].astype(BF16))
        o = _attention(h, bias, attn_sink[l].astype(F32) * LOG2E)
        x = _mixer_out(o, h, x, band, pool_w_groups[l].astype(BF16), pool_scale[l][None],
                       w_proj_attn[l].astype(BF16), w_proj_pool[l].astype(BF16),
                       w_out[l].astype(BF16), ln2_g[l][None], ln2_b[l][None])
        (x,) = _ffn_ln(x, ffn2_w_gate[l].astype(BF16), ffn2_w_up[l].astype(BF16),
                       ffn2_w_down[l].astype(BF16), ln3_g[l][None], ln3_b[l][None],
                       "ffn2_ln3", False)
    return x.reshape(batch, seq, d)
```

```python
import functools
import math

import numpy as np
import jax
import jax.numpy as jnp
from jax import lax
from jax.experimental import pallas as pl
from jax.experimental.pallas import tpu as pltpu

D_MODEL = 2048
SEQ = 16384
HEAD_DIM = 128
N_Q_HEADS = 16
N_KV_HEADS = 4
Q_PER_KV = N_Q_HEADS // N_KV_HEADS
ATTN_WIDTH = N_Q_HEADS * HEAD_DIM
KV_WIDTH = N_KV_HEADS * HEAD_DIM
WINDOW = 128
BLOCK = 128
POOL_WINDOWS = (2, 4, 8, 16)
N_POOL_GROUPS = len(POOL_WINDOWS)
POOL_GROUP_DIM = 256
POOL_WIDTH = N_POOL_GROUPS * POOL_GROUP_DIM
D_FF = 5632
ALPHA = 2.0 ** 0.25
LN_EPS = 1e-5
NEG_INF = -1e30
LOG2E = math.log2(math.e)
IN_WIDTH = ATTN_WIDTH + 2 * KV_WIDTH + POOL_WIDTH + 2 * D_MODEL

Q_COL = 0
K_COL = ATTN_WIDTH // HEAD_DIM
V_COL = (ATTN_WIDTH + KV_WIDTH) // HEAD_DIM
POOL_OFF = ATTN_WIDTH + 2 * KV_WIDTH
GATE_A_OFF = POOL_OFF + POOL_WIDTH
GATE_B_OFF = GATE_A_OFF + D_MODEL

V7X_VMEM_LIMIT_BYTES = 56 * 1024 * 1024
POOL_HALO = 16

FFN_TM = 512
FFN_TF = 512
PROJ_TM = 1024
PROJ_TN = 1024
ATTN_TQ = 1024
MIX_TM = 256

F32 = jnp.float32
BF16 = jnp.bfloat16


def _layer_norm(y, g, b):
    mu = jnp.mean(y, axis=-1, keepdims=True)
    yc = y - mu
    var = jnp.mean(yc * yc, axis=-1, keepdims=True)
    return yc * lax.rsqrt(var + LN_EPS) * g + b


def _sigmoid(x):
    return 1.0 / (1.0 + jnp.exp(-x))


def _ffn_ln_kernel(n_out, n_side, x_ref, wg_ref, wu_ref, wd_ref, g_ref, b_ref, *rest):
    side_in = rest[:n_side]
    out_refs = rest[n_side:n_side + n_out]
    side_out = rest[n_side + n_out:2 * n_side + n_out]
    xb_ref, acc_ref, ypre_ref = rest[2 * n_side + n_out:]
    i = pl.program_id(0)
    j = pl.program_id(1)
    n_tiles = pl.num_programs(0) - 1
    last_j = pl.num_programs(1) - 1

    def down_proj(xb):
        gate = jnp.dot(xb, wg_ref[...], preferred_element_type=F32)
        up = jnp.dot(xb, wu_ref[...], preferred_element_type=F32)
        act = (gate * _sigmoid(gate) * up).astype(BF16)
        return jnp.dot(act, wd_ref[...], preferred_element_type=F32)

    def norm_previous_tile():
        y = _layer_norm(ypre_ref[...], g_ref[...], b_ref[...])
        out_refs[0][...] = y
        if n_out == 2:
            out_refs[1][...] = y.astype(BF16)

    def first_chunk(with_norm):
        if with_norm:
            norm_previous_tile()
        xb = x_ref[...].astype(BF16)
        xb_ref[...] = xb
        acc_ref[...] = down_proj(xb)

    pl.when((j == 0) & (i == 0))(lambda: first_chunk(False))
    pl.when((j == 0) & (i > 0) & (i < n_tiles))(lambda: first_chunk(True))
    pl.when((j == 0) & (i == n_tiles))(norm_previous_tile)

    @pl.when((j > 0) & (j < last_j) & (i < n_tiles))
    def _():
        acc_ref[...] += down_proj(xb_ref[...])

    @pl.when((j == last_j) & (i < n_tiles))
    def _():
        ypre_ref[...] = ALPHA * x_ref[...] + 0.5 * (acc_ref[...] + down_proj(xb_ref[...]))
        for src, dst in zip(side_in, side_out):
            dst[...] = src[...].astype(BF16)


def _ffn_ln(x, wg, wu, wd, g, b, name, with_bf16_copy, side_casts=()):
    s, d = x.shape
    f = wg.shape[1]
    tm, tf = FFN_TM, FFN_TF
    n, nj = s // tm, f // tf
    n_out = 2 if with_bf16_copy else 1

    def wcol(i, j):
        return jnp.where(i < n, j, nj - 1)

    side_specs, side_shapes = [], []
    for w in side_casts:
        rows = w.shape[0] // n
        assert rows * n == w.shape[0] and rows % 16 == 0
        side_specs.append(pl.BlockSpec((rows, w.shape[1]), lambda i, j: (jnp.minimum(i, n - 1), 0)))
        side_shapes.append(jax.ShapeDtypeStruct(w.shape, BF16))
    out_spec = pl.BlockSpec((tm, d), lambda i, j: (jnp.maximum(i - 1, 0), 0))
    return pl.pallas_call(
        functools.partial(_ffn_ln_kernel, n_out, len(side_casts)),
        grid=(n + 1, nj),
        in_specs=[
            pl.BlockSpec((tm, d), lambda i, j: (jnp.minimum(i, n - 1), 0)),
            pl.BlockSpec((d, tf), lambda i, j: (0, wcol(i, j))),
            pl.BlockSpec((d, tf), lambda i, j: (0, wcol(i, j))),
            pl.BlockSpec((tf, d), lambda i, j: (wcol(i, j), 0)),
            pl.BlockSpec((1, d), lambda i, j: (0, 0)),
            pl.BlockSpec((1, d), lambda i, j: (0, 0)),
        ] + side_specs,
        out_specs=[out_spec] * n_out + side_specs,
        out_shape=[jax.ShapeDtypeStruct((s, d), F32),
                   jax.ShapeDtypeStruct((s, d), BF16)][:n_out] + side_shapes,
        scratch_shapes=[pltpu.VMEM((tm, d), BF16), pltpu.VMEM((tm, d), F32),
                        pltpu.VMEM((tm, d), F32)],
        compiler_params=pltpu.CompilerParams(
            dimension_semantics=("arbitrary", "arbitrary"),
            vmem_limit_bytes=V7X_VMEM_LIMIT_BYTES),
        name=name,
    )(x, wg, wu, wd, g, b, *side_casts)


def _proj_kernel(n_side, x_ref, w_ref, *rest):
    side_in = rest[:n_side]
    o_ref = rest[n_side]
    side_out = rest[n_side + 1:]
    o_ref[...] = jnp.dot(x_ref[...], w_ref[...], preferred_element_type=F32).astype(o_ref.dtype)
    for src, dst in zip(side_in, side_out):
        dst[...] = src[...].astype(BF16)


def _in_proj(xb, w, side_casts=()):
    s, d = xb.shape
    n = w.shape[1]
    tm, tn = PROJ_TM, PROJ_TN
    gi, gj = s // tm, n // tn
    steps = gi * gj

    side_specs, side_shapes = [], []
    for a in side_casts:
        n_blocks = steps
        while a.shape[0] % (16 * n_blocks):
            n_blocks //= 2
        assert n_blocks >= 1 and steps % n_blocks == 0
        repeat = steps // n_blocks
        side_specs.append(pl.BlockSpec(
            (a.shape[0] // n_blocks, a.shape[1]),
            functools.partial(lambda i, j, rep: ((i * gj + j) // rep, 0), rep=repeat)))
        side_shapes.append(jax.ShapeDtypeStruct(a.shape, BF16))
    return pl.pallas_call(
        functools.partial(_proj_kernel, len(side_casts)),
        grid=(gi, gj),
        in_specs=[
            pl.BlockSpec((tm, d), lambda i, j: (i, 0)),
            pl.BlockSpec((d, tn), lambda i, j: (0, j)),
        ] + side_specs,
        out_specs=[pl.BlockSpec((tm, tn), lambda i, j: (i, j))] + side_specs,
        out_shape=[jax.ShapeDtypeStruct((s, n), BF16)] + side_shapes,
        compiler_params=pltpu.CompilerParams(
            dimension_semantics=("arbitrary", "arbitrary"),
            vmem_limit_bytes=V7X_VMEM_LIMIT_BYTES),
        name="mixer_in_proj",
    )(xb, w, *side_casts)


def _attn_kernel(sink_ref, q_ref, kp_ref, kc_ref, kn_ref, vp_ref, vc_ref, vn_ref, bias_ref,
                 o_ref, kcat_ref, vcat_ref):
    hk = pl.program_id(0)
    i = pl.program_id(1)
    tq = q_ref.shape[0]
    blocks_per_tile = tq // BLOCK
    n_blocks = pl.num_programs(1) * blocks_per_tile

    kcat_ref[0:BLOCK, :] = kp_ref[...]
    kcat_ref[BLOCK:BLOCK + tq, :] = kc_ref[...]
    kcat_ref[BLOCK + tq:, :] = kn_ref[...]
    vcat_ref[0:BLOCK, 0:HEAD_DIM] = vp_ref[...]
    vcat_ref[BLOCK:BLOCK + tq, 0:HEAD_DIM] = vc_ref[...]
    vcat_ref[BLOCK + tq:, 0:HEAD_DIM] = vn_ref[...]
    vcat_ref[:, HEAD_DIM:] = jnp.ones((tq + 2 * BLOCK, HEAD_DIM), BF16)

    qk_scale = LOG2E / math.sqrt(HEAD_DIM)

    def body(b, carry):
        row = pl.multiple_of(b * BLOCK, BLOCK)
        gb = i * blocks_per_tile + b
        var = jnp.where(gb == 0, 0, jnp.where(gb == n_blocks - 1, 2, 1))
        kw = kcat_ref[pl.ds(row, 3 * BLOCK), :]
        vw = vcat_ref[pl.ds(row, 3 * BLOCK), :]
        for g in range(Q_PER_KV):
            cols = slice(g * HEAD_DIM, (g + 1) * HEAD_DIM)
            sink = sink_ref[hk * Q_PER_KV + g]
            q = q_ref[pl.ds(row, BLOCK), cols]
            s = lax.dot_general(q, kw, (((1,), (1,)), ((), ())), preferred_element_type=F32)
            s = s * qk_scale + bias_ref[0, var, g]
            m = jnp.maximum(jnp.max(s, axis=-1, keepdims=True), sink)
            m_b = jnp.broadcast_to(m, (BLOCK, HEAD_DIM))
            p = jnp.exp2(s - jnp.concatenate([m_b] * 3, axis=-1))
            ov = jnp.dot(p.astype(BF16), vw, preferred_element_type=F32)
            denom = ov[:, HEAD_DIM:] + jnp.exp2(sink - m_b)
            o_ref[pl.ds(row, BLOCK), cols] = (ov[:, :HEAD_DIM] / denom).astype(o_ref.dtype)
        return carry

    lax.fori_loop(0, blocks_per_tile, body, 0, unroll=4)


def _attention(h, bias, sink_log2):
    s = h.shape[0]
    tq = ATTN_TQ
    r = tq // BLOCK
    nb = s // BLOCK
    assert nb >= 2
    gw = Q_PER_KV * HEAD_DIM

    def prev_map(col):
        return lambda hk, i: (jnp.maximum(i * r - 1, 0), col + hk)

    def cur_map(col):
        return lambda hk, i: (i, col + hk)

    def next_map(col):
        return lambda hk, i: (jnp.minimum((i + 1) * r, nb - 1), col + hk)

    return pl.pallas_call(
        _attn_kernel,
        grid=(N_KV_HEADS, s // tq),
        in_specs=[
            pl.BlockSpec(memory_space=pltpu.SMEM),
            pl.BlockSpec((tq, gw), lambda hk, i: (i, hk)),
            pl.BlockSpec((BLOCK, HEAD_DIM), prev_map(K_COL)),
            pl.BlockSpec((tq, HEAD_DIM), cur_map(K_COL)),
            pl.BlockSpec((BLOCK, HEAD_DIM), next_map(K_COL)),
            pl.BlockSpec((BLOCK, HEAD_DIM), prev_map(V_COL)),
            pl.BlockSpec((tq, HEAD_DIM), cur_map(V_COL)),
            pl.BlockSpec((BLOCK, HEAD_DIM), next_map(V_COL)),
            pl.BlockSpec((1, 3, Q_PER_KV, BLOCK, 3 * BLOCK), lambda hk, i: (hk, 0, 0, 0, 0)),
        ],
        out_specs=pl.BlockSpec((tq, gw), lambda hk, i: (i, hk)),
        out_shape=jax.ShapeDtypeStruct((s, ATTN_WIDTH), BF16),
        scratch_shapes=[
            pltpu.VMEM((tq + 2 * BLOCK, HEAD_DIM), BF16),
            pltpu.VMEM((tq + 2 * BLOCK, 2 * HEAD_DIM), BF16),
        ],
        compiler_params=pltpu.CompilerParams(
            dimension_semantics=("parallel", "parallel"),
            vmem_limit_bytes=V7X_VMEM_LIMIT_BYTES),
        name="banded_attention",
    )(sink_log2, h, h, h, h, h, h, h, bias)


def _attn_bias_table():
    a = jnp.arange(BLOCK)[:, None]
    c = jnp.arange(3 * BLOCK)[None, :]
    dist = jnp.abs(a + BLOCK - c)
    slopes = jnp.exp2(-8.0 * jnp.arange(1, N_Q_HEADS + 1, dtype=F32) / N_Q_HEADS)
    bias = -slopes[:, None, None] * dist.astype(F32)[None] * LOG2E
    in_window = dist <= WINDOW
    variants = jnp.stack([in_window & (c >= BLOCK), in_window, in_window & (c < 2 * BLOCK)])
    table = jnp.where(variants[None], bias[:, None], NEG_INF)
    table = table.reshape(N_KV_HEADS, Q_PER_KV, 3, BLOCK, 3 * BLOCK)
    return table.transpose(0, 2, 1, 3, 4)


def _pool_band_matrices(tm):
    t = np.arange(tm)[:, None]
    col = np.arange(tm + 2 * POOL_HALO)[None, :]
    pos = np.where(col < tm + POOL_HALO, col, col - tm - 2 * POOL_HALO)
    mats = []
    for w in POOL_WINDOWS:
        lo = t - w // 2
        hi = t + w - w // 2
        mats.append(((pos >= lo) & (pos < hi)).astype(np.float32))
    return np.stack(mats)


def _mixer_out_kernel(o_ref, pp_ref, pc_ref, pn_ref, ga_ref, gb_ref, x_ref, band_ref, wgrp_ref,
                      pscale_ref, wpa_ref, wpp_ref, wout_ref, g_ref, b_ref,
                      out_ref, pcat_ref):
    i = pl.program_id(0)
    tm = pc_ref.shape[0]
    seq = pl.num_programs(0) * tm

    zeros_halo = jnp.zeros((POOL_HALO, POOL_WIDTH), BF16)
    pcat_ref[0:tm, :] = pc_ref[...]
    pcat_ref[tm:tm + POOL_HALO, :] = jnp.where(i == pl.num_programs(0) - 1, zeros_halo, pn_ref[...])
    pcat_ref[tm + POOL_HALO:, :] = jnp.where(i == 0, zeros_halo, pp_ref[...])

    t = i * tm + lax.broadcasted_iota(jnp.int32, (tm, 1), 0)
    mixed = []
    for g, w in enumerate(POOL_WINDOWS):
        cols = slice(g * POOL_GROUP_DIM, (g + 1) * POOL_GROUP_DIM)
        pg = pcat_ref[:, cols]
        win_sum = jnp.dot(band_ref[g], pg, preferred_element_type=F32)
        lo = jnp.clip(t - w // 2, 0, seq)
        hi = jnp.clip(t + w - w // 2, 0, seq)
        mean = win_sum / (hi - lo).astype(F32)
        d = (mean - pc_ref[:, cols].astype(F32)).astype(BF16)
        mg = jnp.dot(d, wgrp_ref[g], preferred_element_type=F32) * pscale_ref[:, cols]
        mixed.append(mg.astype(BF16))
    mixed = jnp.concatenate(mixed, axis=-1)

    y_a = jnp.dot(o_ref[...], wpa_ref[...], preferred_element_type=F32)
    y_b = jnp.dot(mixed, wpp_ref[...], preferred_element_type=F32)
    merged = _sigmoid(ga_ref[...].astype(F32)) * y_a + _sigmoid(gb_ref[...].astype(F32)) * y_b
    z = jnp.dot(merged.astype(BF16), wout_ref[...], preferred_element_type=F32)
    out_ref[...] = _layer_norm(ALPHA * x_ref[...] + z, g_ref[...], b_ref[...])


def _mixer_out(o, h, x, band, wgrp, pscale, wpa, wpp, wout, g, b):
    s, d = x.shape
    tm = MIX_TM
    hr = tm // POOL_HALO
    n_halo = s // POOL_HALO
    pool_cb = POOL_OFF // POOL_WIDTH
    ga_cb = GATE_A_OFF // D_MODEL
    gb_cb = GATE_B_OFF // D_MODEL
    assert POOL_OFF % POOL_WIDTH == 0 and GATE_A_OFF % D_MODEL == 0 and GATE_B_OFF % D_MODEL == 0

    def const(shape):
        return pl.BlockSpec(shape, lambda i: (0,) * len(shape), pipeline_mode=pl.Buffered(1))

    return pl.pallas_call(
        _mixer_out_kernel,
        grid=(s // tm,),
        in_specs=[
            pl.BlockSpec((tm, ATTN_WIDTH), lambda i: (i, 0)),
            pl.BlockSpec((POOL_HALO, POOL_WIDTH), lambda i: (jnp.maximum(i * hr - 1, 0), pool_cb)),
            pl.BlockSpec((tm, POOL_WIDTH), lambda i: (i, pool_cb)),
            pl.BlockSpec((POOL_HALO, POOL_WIDTH),
                         lambda i: (jnp.minimum((i + 1) * hr, n_halo - 1), pool_cb)),
            pl.BlockSpec((tm, D_MODEL), lambda i: (i, ga_cb)),
            pl.BlockSpec((tm, D_MODEL), lambda i: (i, gb_cb)),
            pl.BlockSpec((tm, d), lambda i: (i, 0)),
            const(band.shape),
            const(wgrp.shape),
            const(pscale.shape),
            const(wpa.shape),
            const(wpp.shape),
            const(wout.shape),
            const(g.shape),
            const(b.shape),
        ],
        out_specs=pl.BlockSpec((tm, d), lambda i: (i, 0)),
        out_shape=jax.ShapeDtypeStruct((s, d), F32),
        scratch_shapes=[pltpu.VMEM((tm + 2 * POOL_HALO, POOL_WIDTH), BF16)],
        compiler_params=pltpu.CompilerParams(
            dimension_semantics=("parallel",),
            vmem_limit_bytes=V7X_VMEM_LIMIT_BYTES),
        name="mixer_out_ln",
    )(o, h, h, h, h, h, x, band, wgrp, pscale, wpa, wpp, wout, g, b)


def kernel(x, ffn1_w_gate, ffn1_w_up, ffn1_w_down, ln1_g, ln1_b, w_in, attn_sink, pool_w_groups,
           pool_scale, w_proj_attn, w_proj_pool, w_out, ln2_g, ln2_b, ffn2_w_gate, ffn2_w_up,
           ffn2_w_down, ln3_g, ln3_b):
    batch, seq, d = x.shape
    assert (batch, seq, d) == (1, SEQ, D_MODEL)
    x = x.reshape(seq, d)

    bias = _attn_bias_table()
    band = jnp.asarray(_pool_band_matrices(MIX_TM), BF16)
    depth = ffn1_w_gate.shape[0]
    for l in range(depth):
        x, xb, w_in_b = _ffn_ln(
            x, ffn1_w_gate[l].astype(BF16), ffn1_w_up[l].astype(BF16),
            ffn1_w_down[l].astype(BF16), ln1_g[l][None], ln1_b[l][None], "ffn1_ln1", True,
            side_casts=(w_in[l],))
        h, wg2, wu2, wd2, wpa, wpp, wout = _in_proj(
            xb, w_in_b, side_casts=(ffn2_w_gate[l], ffn2_w_up[l], ffn2_w_down[l],
                                    w_proj_attn[l], w_proj_pool[l], w_out[l]))
        o = _attention(h, bias, attn_sink[l].astype(F32) * LOG2E)
        x = _mixer_out(o, h, x, band, pool_w_groups[l].astype(BF16), pool_scale[l][None],
                       wpa, wpp, wout, ln2_g[l][None], ln2_b[l][None])
        (x,) = _ffn_ln(x, wg2, wu2, wd2, ln3_g[l][None], ln3_b[l][None], "ffn2_ln3", False)
    return x.reshape(batch, seq, d)
```

```python
import functools
import math

import numpy as np
import jax
import jax.numpy as jnp
from jax import lax
from jax.experimental import pallas as pl
from jax.experimental.pallas import tpu as pltpu

D_MODEL = 2048
SEQ = 16384
HEAD_DIM = 128
N_Q_HEADS = 16
N_KV_HEADS = 4
Q_PER_KV = N_Q_HEADS // N_KV_HEADS
ATTN_WIDTH = N_Q_HEADS * HEAD_DIM
KV_WIDTH = N_KV_HEADS * HEAD_DIM
WINDOW = 128
BLOCK = 128
POOL_WINDOWS = (2, 4, 8, 16)
N_POOL_GROUPS = len(POOL_WINDOWS)
POOL_GROUP_DIM = 256
POOL_WIDTH = N_POOL_GROUPS * POOL_GROUP_DIM
D_FF = 5632
ALPHA = 2.0 ** 0.25
LN_EPS = 1e-5
NEG_INF = -1e30
LOG2E = math.log2(math.e)
IN_WIDTH = ATTN_WIDTH + 2 * KV_WIDTH + POOL_WIDTH + 2 * D_MODEL

Q_COL = 0
K_COL = ATTN_WIDTH // HEAD_DIM
V_COL = (ATTN_WIDTH + KV_WIDTH) // HEAD_DIM
POOL_OFF = ATTN_WIDTH + 2 * KV_WIDTH
GATE_A_OFF = POOL_OFF + POOL_WIDTH
GATE_B_OFF = GATE_A_OFF + D_MODEL

V7X_VMEM_LIMIT_BYTES = 56 * 1024 * 1024
POOL_HALO = 16

FFN_TM = 512
FFN_TF = 512
PROJ_TM = 2048
PROJ_TN = 1024
ATTN_TQ = 1024
MIX_TM = 256

F32 = jnp.float32
BF16 = jnp.bfloat16


def _layer_norm(y, g, b):
    mu = jnp.mean(y, axis=-1, keepdims=True)
    yc = y - mu
    var = jnp.mean(yc * yc, axis=-1, keepdims=True)
    return yc * lax.rsqrt(var + LN_EPS) * g + b


def _sigmoid(x):
    return 1.0 / (1.0 + jnp.exp(-x))


def _ffn_ln_kernel(n_out, n_side, x_ref, wgu_ref, wd_ref, g_ref, b_ref, *rest):
    side_in = rest[:n_side]
    out_refs = rest[n_side:n_side + n_out]
    side_out = rest[n_side + n_out:2 * n_side + n_out]
    xb_ref, acc_ref, ypre_ref = rest[2 * n_side + n_out:]
    i = pl.program_id(0)
    j = pl.program_id(1)
    n_tiles = pl.num_programs(0) - 1
    last_j = pl.num_programs(1) - 1

    def down_proj(xb):
        tf = wd_ref.shape[0]
        gate_up = jnp.dot(xb, wgu_ref[0], preferred_element_type=F32)
        gate, up = gate_up[:, :tf], gate_up[:, tf:]
        act = (gate * _sigmoid(gate) * up).astype(BF16)
        return jnp.dot(act, wd_ref[...], preferred_element_type=F32)

    def norm_previous_tile():
        y = _layer_norm(ypre_ref[...], g_ref[...], b_ref[...])
        out_refs[0][...] = y
        if n_out == 2:
            out_refs[1][...] = y.astype(BF16)

    def first_chunk(with_norm):
        if with_norm:
            norm_previous_tile()
        xb = x_ref[...].astype(BF16)
        xb_ref[...] = xb
        acc_ref[...] = down_proj(xb)

    pl.when((j == 0) & (i == 0))(lambda: first_chunk(False))
    pl.when((j == 0) & (i > 0) & (i < n_tiles))(lambda: first_chunk(True))
    pl.when((j == 0) & (i == n_tiles))(norm_previous_tile)

    @pl.when((j > 0) & (j < last_j) & (i < n_tiles))
    def _():
        acc_ref[...] += down_proj(xb_ref[...])

    @pl.when((j == last_j) & (i < n_tiles))
    def _():
        ypre_ref[...] = ALPHA * x_ref[...] + 0.5 * (acc_ref[...] + down_proj(xb_ref[...]))
        for src, dst in zip(side_in, side_out):
            dst[...] = src[...].astype(BF16)


def _chunked_gate_up(gate, up):
    d, f = gate.shape
    nj = f // FFN_TF
    slabs = jnp.concatenate([gate.reshape(d, nj, FFN_TF), up.reshape(d, nj, FFN_TF)], axis=2)
    return slabs.transpose(1, 0, 2).astype(BF16)


def _ffn_ln(x, wgu, wd, g, b, name, with_bf16_copy, side_casts=()):
    s, d = x.shape
    tm, tf = FFN_TM, FFN_TF
    nj = wgu.shape[0]
    assert wgu.shape == (nj, d, 2 * tf) and wd.shape == (nj * tf, d)
    n = s // tm
    n_out = 2 if with_bf16_copy else 1

    def wcol(i, j):
        return jnp.where(i < n, j, nj - 1)

    side_specs, side_shapes = [], []
    for w in side_casts:
        rows = w.shape[0] // n
        assert rows * n == w.shape[0] and rows % 16 == 0
        side_specs.append(pl.BlockSpec((rows, w.shape[1]), lambda i, j: (jnp.minimum(i, n - 1), 0)))
        side_shapes.append(jax.ShapeDtypeStruct(w.shape, BF16))
    out_spec = pl.BlockSpec((tm, d), lambda i, j: (jnp.maximum(i - 1, 0), 0))
    return pl.pallas_call(
        functools.partial(_ffn_ln_kernel, n_out, len(side_casts)),
        grid=(n + 1, nj),
        in_specs=[
            pl.BlockSpec((tm, d), lambda i, j: (jnp.minimum(i, n - 1), 0)),
            pl.BlockSpec((1, d, 2 * tf), lambda i, j: (wcol(i, j), 0, 0)),
            pl.BlockSpec((tf, d), lambda i, j: (wcol(i, j), 0)),
            pl.BlockSpec((1, d), lambda i, j: (0, 0)),
            pl.BlockSpec((1, d), lambda i, j: (0, 0)),
        ] + side_specs,
        out_specs=[out_spec] * n_out + side_specs,
        out_shape=[jax.ShapeDtypeStruct((s, d), F32),
                   jax.ShapeDtypeStruct((s, d), BF16)][:n_out] + side_shapes,
        scratch_shapes=[pltpu.VMEM((tm, d), BF16), pltpu.VMEM((tm, d), F32),
                        pltpu.VMEM((tm, d), F32)],
        compiler_params=pltpu.CompilerParams(
            dimension_semantics=("arbitrary", "arbitrary"),
            vmem_limit_bytes=V7X_VMEM_LIMIT_BYTES),
        name=name,
    )(x, wgu, wd, g, b, *side_casts)


def _proj_kernel(repeats, x_ref, w_ref, gate_ref, up_ref, *rest):
    n_side = len(repeats)
    side_in = rest[:n_side]
    o_ref, wgu_ref = rest[n_side], rest[n_side + 1]
    side_out = rest[n_side + 2:]
    step = pl.program_id(0) * pl.num_programs(1) + pl.program_id(1)

    o_ref[...] = jnp.dot(x_ref[...], w_ref[...], preferred_element_type=F32).astype(o_ref.dtype)

    tf = wgu_ref.shape[2] // 2
    for c in range(wgu_ref.shape[0]):
        wgu_ref[c, :, :tf] = gate_ref[:, c * tf:(c + 1) * tf].astype(BF16)
        wgu_ref[c, :, tf:] = up_ref[:, c * tf:(c + 1) * tf].astype(BF16)
    for src, dst, rep in zip(side_in, side_out, repeats):
        if rep == 1:
            dst[...] = src[...].astype(BF16)
        else:
            @pl.when(step % rep == 0)
            def _(src=src, dst=dst):
                dst[...] = src[...].astype(BF16)


def _in_proj(xb, w, ffn_gate, ffn_up, side_casts):
    s, d = xb.shape
    n = w.shape[1]
    tm, tn = PROJ_TM, PROJ_TN
    gi, gj = s // tm, n // tn
    steps = gi * gj
    tf = FFN_TF
    f = ffn_gate.shape[1]
    nj = f // tf
    gu_rows = d // steps
    assert gu_rows * steps == d and gu_rows % 16 == 0 and nj * tf == f

    def step_map(rep, ndim):
        return lambda i, j: ((i * gj + j) // rep,) + (0,) * (ndim - 1)

    side_specs, side_shapes, repeats = [], [], []
    for a in side_casts:
        n_blocks = steps
        while a.shape[0] % (16 * n_blocks):
            n_blocks //= 2
        assert n_blocks >= 1 and steps % n_blocks == 0
        repeats.append(steps // n_blocks)
        side_specs.append(pl.BlockSpec((a.shape[0] // n_blocks, a.shape[1]),
                                       step_map(repeats[-1], 2)))
        side_shapes.append(jax.ShapeDtypeStruct(a.shape, BF16))
    gu_spec = pl.BlockSpec((gu_rows, f), step_map(1, 2))
    return pl.pallas_call(
        functools.partial(_proj_kernel, tuple(repeats)),
        grid=(gi, gj),
        in_specs=[
            pl.BlockSpec((tm, d), lambda i, j: (i, 0)),
            pl.BlockSpec((d, tn), lambda i, j: (0, j)),
            gu_spec, gu_spec,
        ] + side_specs,
        out_specs=[pl.BlockSpec((tm, tn), lambda i, j: (i, j)),
                   pl.BlockSpec((nj, gu_rows, 2 * tf), lambda i, j: (0, i * gj + j, 0))] + side_specs,
        out_shape=[jax.ShapeDtypeStruct((s, n), BF16),
                   jax.ShapeDtypeStruct((nj, d, 2 * tf), BF16)] + side_shapes,
        compiler_params=pltpu.CompilerParams(
            dimension_semantics=("arbitrary", "arbitrary"),
            vmem_limit_bytes=V7X_VMEM_LIMIT_BYTES),
        name="mixer_in_proj",
    )(xb, w, ffn_gate, ffn_up, *side_casts)


def _attn_kernel(sink_ref, q_ref, kp_ref, kc_ref, kn_ref, vp_ref, vc_ref, vn_ref, bias_ref,
                 o_ref, kcat_ref, vcat_ref):
    hk = pl.program_id(0)
    i = pl.program_id(1)
    tq = q_ref.shape[0]
    blocks_per_tile = tq // BLOCK
    n_blocks = pl.num_programs(1) * blocks_per_tile

    kcat_ref[0:BLOCK, :] = kp_ref[...]
    kcat_ref[BLOCK:BLOCK + tq, :] = kc_ref[...]
    kcat_ref[BLOCK + tq:, :] = kn_ref[...]
    vcat_ref[0:BLOCK, 0:HEAD_DIM] = vp_ref[...]
    vcat_ref[BLOCK:BLOCK + tq, 0:HEAD_DIM] = vc_ref[...]
    vcat_ref[BLOCK + tq:, 0:HEAD_DIM] = vn_ref[...]
    vcat_ref[:, HEAD_DIM:] = jnp.ones((tq + 2 * BLOCK, HEAD_DIM), BF16)

    qk_scale = LOG2E / math.sqrt(HEAD_DIM)

    def body(b, carry):
        row = pl.multiple_of(b * BLOCK, BLOCK)
        gb = i * blocks_per_tile + b
        var = jnp.where(gb == 0, 0, jnp.where(gb == n_blocks - 1, 2, 1))
        kw = kcat_ref[pl.ds(row, 3 * BLOCK), :]
        vw = vcat_ref[pl.ds(row, 3 * BLOCK), :]
        for g in range(Q_PER_KV):
            cols = slice(g * HEAD_DIM, (g + 1) * HEAD_DIM)
            sink = sink_ref[hk * Q_PER_KV + g]
            q = q_ref[pl.ds(row, BLOCK), cols]
            s = lax.dot_general(q, kw, (((1,), (1,)), ((), ())), preferred_element_type=F32)
            s = s * qk_scale + bias_ref[0, var, g]
            m = jnp.maximum(jnp.max(s, axis=-1, keepdims=True), sink)
            m_b = jnp.broadcast_to(m, (BLOCK, HEAD_DIM))
            p = jnp.exp2(s - jnp.concatenate([m_b] * 3, axis=-1))
            ov = jnp.dot(p.astype(BF16), vw, preferred_element_type=F32)
            denom = ov[:, HEAD_DIM:] + jnp.exp2(sink - m_b)
            o_ref[pl.ds(row, BLOCK), cols] = (ov[:, :HEAD_DIM] / denom).astype(o_ref.dtype)
        return carry

    lax.fori_loop(0, blocks_per_tile, body, 0, unroll=4)


def _attention(h, bias, sink_log2):
    s = h.shape[0]
    tq = ATTN_TQ
    r = tq // BLOCK
    nb = s // BLOCK
    assert nb >= 2
    gw = Q_PER_KV * HEAD_DIM

    def prev_map(col):
        return lambda hk, i: (jnp.maximum(i * r - 1, 0), col + hk)

    def cur_map(col):
        return lambda hk, i: (i, col + hk)

    def next_map(col):
        return lambda hk, i: (jnp.minimum((i + 1) * r, nb - 1), col + hk)

    return pl.pallas_call(
        _attn_kernel,
        grid=(N_KV_HEADS, s // tq),
        in_specs=[
            pl.BlockSpec(memory_space=pltpu.SMEM),
            pl.BlockSpec((tq, gw), lambda hk, i: (i, hk)),
            pl.BlockSpec((BLOCK, HEAD_DIM), prev_map(K_COL)),
            pl.BlockSpec((tq, HEAD_DIM), cur_map(K_COL)),
            pl.BlockSpec((BLOCK, HEAD_DIM), next_map(K_COL)),
            pl.BlockSpec((BLOCK, HEAD_DIM), prev_map(V_COL)),
            pl.BlockSpec((tq, HEAD_DIM), cur_map(V_COL)),
            pl.BlockSpec((BLOCK, HEAD_DIM), next_map(V_COL)),
            pl.BlockSpec((1, 3, Q_PER_KV, BLOCK, 3 * BLOCK), lambda hk, i: (hk, 0, 0, 0, 0)),
        ],
        out_specs=pl.BlockSpec((tq, gw), lambda hk, i: (i, hk)),
        out_shape=jax.ShapeDtypeStruct((s, ATTN_WIDTH), BF16),
        scratch_shapes=[
            pltpu.VMEM((tq + 2 * BLOCK, HEAD_DIM), BF16),
            pltpu.VMEM((tq + 2 * BLOCK, 2 * HEAD_DIM), BF16),
        ],
        compiler_params=pltpu.CompilerParams(
            dimension_semantics=("parallel", "parallel"),
            vmem_limit_bytes=V7X_VMEM_LIMIT_BYTES),
        name="banded_attention",
    )(sink_log2, h, h, h, h, h, h, h, bias)


def _attn_bias_table():
    a = jnp.arange(BLOCK)[:, None]
    c = jnp.arange(3 * BLOCK)[None, :]
    dist = jnp.abs(a + BLOCK - c)
    slopes = jnp.exp2(-8.0 * jnp.arange(1, N_Q_HEADS + 1, dtype=F32) / N_Q_HEADS)
    bias = -slopes[:, None, None] * dist.astype(F32)[None] * LOG2E
    in_window = dist <= WINDOW
    variants = jnp.stack([in_window & (c >= BLOCK), in_window, in_window & (c < 2 * BLOCK)])
    table = jnp.where(variants[None], bias[:, None], NEG_INF)
    table = table.reshape(N_KV_HEADS, Q_PER_KV, 3, BLOCK, 3 * BLOCK)
    return table.transpose(0, 2, 1, 3, 4)


def _pool_band_matrices(tm):
    t = np.arange(tm)[:, None]
    col = np.arange(tm + 2 * POOL_HALO)[None, :]
    pos = np.where(col < tm + POOL_HALO, col, col - tm - 2 * POOL_HALO)
    mats = []
    for w in POOL_WINDOWS:
        lo = t - w // 2
        hi = t + w - w // 2
        mats.append(((pos >= lo) & (pos < hi)).astype(np.float32))
    return np.stack(mats)


def _mixer_out_kernel(o_ref, pp_ref, pc_ref, pn_ref, ga_ref, gb_ref, x_ref, band_ref, wgrp_ref,
                      pscale_ref, wpa_ref, wpp_ref, wout_ref, g_ref, b_ref,
                      out_ref, pcat_ref):
    i = pl.program_id(0)
    tm = pc_ref.shape[0]
    seq = pl.num_programs(0) * tm

    zeros_halo = jnp.zeros((POOL_HALO, POOL_WIDTH), BF16)
    pcat_ref[0:tm, :] = pc_ref[...]
    pcat_ref[tm:tm + POOL_HALO, :] = jnp.where(i == pl.num_programs(0) - 1, zeros_halo, pn_ref[...])
    pcat_ref[tm + POOL_HALO:, :] = jnp.where(i == 0, zeros_halo, pp_ref[...])

    t = i * tm + lax.broadcasted_iota(jnp.int32, (tm, 1), 0)
    mixed = []
    for g, w in enumerate(POOL_WINDOWS):
        cols = slice(g * POOL_GROUP_DIM, (g + 1) * POOL_GROUP_DIM)
        pg = pcat_ref[:, cols]
        win_sum = jnp.dot(band_ref[g], pg, preferred_element_type=F32)
        lo = jnp.clip(t - w // 2, 0, seq)
        hi = jnp.clip(t + w - w // 2, 0, seq)
        mean = win_sum / (hi - lo).astype(F32)
        d = (mean - pc_ref[:, cols].astype(F32)).astype(BF16)
        mg = jnp.dot(d, wgrp_ref[g], preferred_element_type=F32) * pscale_ref[:, cols]
        mixed.append(mg.astype(BF16))
    mixed = jnp.concatenate(mixed, axis=-1)

    y_a = jnp.dot(o_ref[...], wpa_ref[...], preferred_element_type=F32)
    y_b = jnp.dot(mixed, wpp_ref[...], preferred_element_type=F32)
    merged = _sigmoid(ga_ref[...].astype(F32)) * y_a + _sigmoid(gb_ref[...].astype(F32)) * y_b
    z = jnp.dot(merged.astype(BF16), wout_ref[...], preferred_element_type=F32)
    out_ref[...] = _layer_norm(ALPHA * x_ref[...] + z, g_ref[...], b_ref[...])


def _mixer_out(o, h, x, band, wgrp, pscale, wpa, wpp, wout, g, b):
    s, d = x.shape
    tm = MIX_TM
    hr = tm // POOL_HALO
    n_halo = s // POOL_HALO
    pool_cb = POOL_OFF // POOL_WIDTH
    ga_cb = GATE_A_OFF // D_MODEL
    gb_cb = GATE_B_OFF // D_MODEL
    assert POOL_OFF % POOL_WIDTH == 0 and GATE_A_OFF % D_MODEL == 0 and GATE_B_OFF % D_MODEL == 0

    def const(shape):
        return pl.BlockSpec(shape, lambda i: (0,) * len(shape), pipeline_mode=pl.Buffered(1))

    return pl.pallas_call(
        _mixer_out_kernel,
        grid=(s // tm,),
        in_specs=[
            pl.BlockSpec((tm, ATTN_WIDTH), lambda i: (i, 0)),
            pl.BlockSpec((POOL_HALO, POOL_WIDTH), lambda i: (jnp.maximum(i * hr - 1, 0), pool_cb)),
            pl.BlockSpec((tm, POOL_WIDTH), lambda i: (i, pool_cb)),
            pl.BlockSpec((POOL_HALO, POOL_WIDTH),
                         lambda i: (jnp.minimum((i + 1) * hr, n_halo - 1), pool_cb)),
            pl.BlockSpec((tm, D_MODEL), lambda i: (i, ga_cb)),
            pl.BlockSpec((tm, D_MODEL), lambda i: (i, gb_cb)),
            pl.BlockSpec((tm, d), lambda i: (i, 0)),
            const(band.shape),
            const(wgrp.shape),
            const(pscale.shape),
            const(wpa.shape),
            const(wpp.shape),
            const(wout.shape),
            const(g.shape),
            const(b.shape),
        ],
        out_specs=pl.BlockSpec((tm, d), lambda i: (i, 0)),
        out_shape=jax.ShapeDtypeStruct((s, d), F32),
        scratch_shapes=[pltpu.VMEM((tm + 2 * POOL_HALO, POOL_WIDTH), BF16)],
        compiler_params=pltpu.CompilerParams(
            dimension_semantics=("parallel",),
            vmem_limit_bytes=V7X_VMEM_LIMIT_BYTES),
        name="mixer_out_ln",
    )(o, h, h, h, h, h, x, band, wgrp, pscale, wpa, wpp, wout, g, b)


def kernel(x, ffn1_w_gate, ffn1_w_up, ffn1_w_down, ln1_g, ln1_b, w_in, attn_sink, pool_w_groups,
           pool_scale, w_proj_attn, w_proj_pool, w_out, ln2_g, ln2_b, ffn2_w_gate, ffn2_w_up,
           ffn2_w_down, ln3_g, ln3_b):
    batch, seq, d = x.shape
    assert (batch, seq, d) == (1, SEQ, D_MODEL)
    x = x.reshape(seq, d)

    bias = _attn_bias_table()
    band = jnp.asarray(_pool_band_matrices(MIX_TM), BF16)
    depth = ffn1_w_gate.shape[0]
    for l in range(depth):
        x, xb, w_in_b = _ffn_ln(
            x, _chunked_gate_up(ffn1_w_gate[l], ffn1_w_up[l]), ffn1_w_down[l].astype(BF16),
            ln1_g[l][None], ln1_b[l][None], "ffn1_ln1", True, side_casts=(w_in[l],))
        h, wgu2, wd2, wpa, wpp, wout = _in_proj(
            xb, w_in_b, ffn2_w_gate[l], ffn2_w_up[l],
            side_casts=(ffn2_w_down[l], w_proj_attn[l], w_proj_pool[l], w_out[l]))
        o = _attention(h, bias, attn_sink[l].astype(F32) * LOG2E)
        x = _mixer_out(o, h, x, band, pool_w_groups[l].astype(BF16), pool_scale[l][None],
                       wpa, wpp, wout, ln2_g[l][None], ln2_b[l][None])
        (x,) = _ffn_ln(x, wgu2, wd2, ln3_g[l][None], ln3_b[l][None], "ffn2_ln3", False)
    return x.reshape(batch, seq, d)
```

```python
import functools
import math

import numpy as np
import jax
import jax.numpy as jnp
from jax import lax
from jax.experimental import pallas as pl
from jax.experimental.pallas import tpu as pltpu

D_MODEL = 2048
SEQ = 16384
HEAD_DIM = 128
N_Q_HEADS = 16
N_KV_HEADS = 4
Q_PER_KV = N_Q_HEADS // N_KV_HEADS
ATTN_WIDTH = N_Q_HEADS * HEAD_DIM
KV_WIDTH = N_KV_HEADS * HEAD_DIM
WINDOW = 128
BLOCK = 128
POOL_WINDOWS = (2, 4, 8, 16)
N_POOL_GROUPS = len(POOL_WINDOWS)
POOL_GROUP_DIM = 256
POOL_WIDTH = N_POOL_GROUPS * POOL_GROUP_DIM
D_FF = 5632
ALPHA = 2.0 ** 0.25
LN_EPS = 1e-5
NEG_INF = -1e30
LOG2E = math.log2(math.e)
IN_WIDTH = ATTN_WIDTH + 2 * KV_WIDTH + POOL_WIDTH + 2 * D_MODEL

Q_COL = 0
K_COL = ATTN_WIDTH // HEAD_DIM
V_COL = (ATTN_WIDTH + KV_WIDTH) // HEAD_DIM
POOL_OFF = ATTN_WIDTH + 2 * KV_WIDTH
GATE_A_OFF = POOL_OFF + POOL_WIDTH
GATE_B_OFF = GATE_A_OFF + D_MODEL

V7X_VMEM_LIMIT_BYTES = 56 * 1024 * 1024
POOL_HALO = 16

FFN_TM = 512
FFN_TF = 512
PROJ_TM = 2048
PROJ_TN = 1024
ATTN_TQ = 1024
MIX_TM = 256

F32 = jnp.float32
BF16 = jnp.bfloat16


def _layer_norm(y, g, b):
    mu = jnp.mean(y, axis=-1, keepdims=True)
    yc = y - mu
    var = jnp.mean(yc * yc, axis=-1, keepdims=True)
    return yc * lax.rsqrt(var + LN_EPS) * g + b


def _sigmoid(x):
    return 1.0 / (1.0 + jnp.exp(-x))


def _ffn_ln_kernel(n_out, n_side, x_ref, wg_ref, wu_ref, wd_ref, g_ref, b_ref, *rest):
    side_in = rest[:n_side]
    out_refs = rest[n_side:n_side + n_out]
    side_out = rest[n_side + n_out:2 * n_side + n_out]
    xb_ref, acc_ref, ypre_ref = rest[2 * n_side + n_out:]
    i = pl.program_id(0)
    j = pl.program_id(1)
    n_tiles = pl.num_programs(0) - 1
    last_j = pl.num_programs(1) - 1

    def down_proj(xb):
        gate = jnp.dot(xb, wg_ref[...], preferred_element_type=F32)
        up = jnp.dot(xb, wu_ref[...], preferred_element_type=F32)
        act = (gate * _sigmoid(gate) * up).astype(BF16)
        return jnp.dot(act, wd_ref[...], preferred_element_type=F32)

    def norm_previous_tile():
        y = _layer_norm(ypre_ref[...], g_ref[...], b_ref[...])
        out_refs[0][...] = y
        if n_out == 2:
            out_refs[1][...] = y.astype(BF16)

    def first_chunk(with_norm):
        if with_norm:
            norm_previous_tile()
        xb = x_ref[...].astype(BF16)
        xb_ref[...] = xb
        acc_ref[...] = down_proj(xb)

    pl.when((j == 0) & (i == 0))(lambda: first_chunk(False))
    pl.when((j == 0) & (i > 0) & (i < n_tiles))(lambda: first_chunk(True))
    pl.when((j == 0) & (i == n_tiles))(norm_previous_tile)

    @pl.when((j > 0) & (j < last_j) & (i < n_tiles))
    def _():
        acc_ref[...] += down_proj(xb_ref[...])

    @pl.when((j == last_j) & (i < n_tiles))
    def _():
        ypre_ref[...] = ALPHA * x_ref[...] + 0.5 * (acc_ref[...] + down_proj(xb_ref[...]))
        for src, dst in zip(side_in, side_out):
            dst[...] = src[...].astype(BF16)


def _ffn_ln(x, wg, wu, wd, g, b, name, with_bf16_copy, side_casts=()):
    s, d = x.shape
    f = wg.shape[1]
    tm, tf = FFN_TM, FFN_TF
    n, nj = s // tm, f // tf
    n_out = 2 if with_bf16_copy else 1

    def wcol(i, j):
        return jnp.where(i < n, j, nj - 1)

    side_specs, side_shapes = [], []
    for w in side_casts:
        rows = w.shape[0] // n
        assert rows * n == w.shape[0] and rows % 16 == 0
        side_specs.append(pl.BlockSpec((rows, w.shape[1]), lambda i, j: (jnp.minimum(i, n - 1), 0)))
        side_shapes.append(jax.ShapeDtypeStruct(w.shape, BF16))
    out_spec = pl.BlockSpec((tm, d), lambda i, j: (jnp.maximum(i - 1, 0), 0))
    return pl.pallas_call(
        functools.partial(_ffn_ln_kernel, n_out, len(side_casts)),
        grid=(n + 1, nj),
        in_specs=[
            pl.BlockSpec((tm, d), lambda i, j: (jnp.minimum(i, n - 1), 0)),
            pl.BlockSpec((d, tf), lambda i, j: (0, wcol(i, j))),
            pl.BlockSpec((d, tf), lambda i, j: (0, wcol(i, j))),
            pl.BlockSpec((tf, d), lambda i, j: (wcol(i, j), 0)),
            pl.BlockSpec((1, d), lambda i, j: (0, 0)),
            pl.BlockSpec((1, d), lambda i, j: (0, 0)),
        ] + side_specs,
        out_specs=[out_spec] * n_out + side_specs,
        out_shape=[jax.ShapeDtypeStruct((s, d), F32),
                   jax.ShapeDtypeStruct((s, d), BF16)][:n_out] + side_shapes,
        scratch_shapes=[pltpu.VMEM((tm, d), BF16), pltpu.VMEM((tm, d), F32),
                        pltpu.VMEM((tm, d), F32)],
        compiler_params=pltpu.CompilerParams(
            dimension_semantics=("arbitrary", "arbitrary"),
            vmem_limit_bytes=V7X_VMEM_LIMIT_BYTES),
        name=name,
    )(x, wg, wu, wd, g, b, *side_casts)


def _proj_kernel(repeats, x_ref, w_ref, *rest):
    n_side = len(repeats)
    side_in = rest[:n_side]
    o_ref = rest[n_side]
    side_out = rest[n_side + 1:]
    step = pl.program_id(0) * pl.num_programs(1) + pl.program_id(1)

    o_ref[...] = jnp.dot(x_ref[...], w_ref[...], preferred_element_type=F32).astype(o_ref.dtype)

    for src, dst, rep in zip(side_in, side_out, repeats):
        if rep == 1:
            dst[...] = src[...].astype(BF16)
        else:
            @pl.when(step % rep == 0)
            def _(src=src, dst=dst):
                dst[...] = src[...].astype(BF16)


def _in_proj(xb, w, side_casts):
    s, d = xb.shape
    n = w.shape[1]
    tm, tn = PROJ_TM, PROJ_TN
    gi, gj = s // tm, n // tn
    steps = gi * gj

    def step_map(rep):
        return lambda i, j: ((i * gj + j) // rep, 0)

    side_specs, side_shapes, repeats = [], [], []
    for a in side_casts:
        n_blocks = steps
        while a.shape[0] % (16 * n_blocks):
            n_blocks //= 2
        assert n_blocks >= 1 and steps % n_blocks == 0
        repeats.append(steps // n_blocks)
        side_specs.append(pl.BlockSpec((a.shape[0] // n_blocks, a.shape[1]),
                                       step_map(repeats[-1])))
        side_shapes.append(jax.ShapeDtypeStruct(a.shape, BF16))
    return pl.pallas_call(
        functools.partial(_proj_kernel, tuple(repeats)),
        grid=(gi, gj),
        in_specs=[
            pl.BlockSpec((tm, d), lambda i, j: (i, 0)),
            pl.BlockSpec((d, tn), lambda i, j: (0, j)),
        ] + side_specs,
        out_specs=[pl.BlockSpec((tm, tn), lambda i, j: (i, j))] + side_specs,
        out_shape=[jax.ShapeDtypeStruct((s, n), BF16)] + side_shapes,
        compiler_params=pltpu.CompilerParams(
            dimension_semantics=("arbitrary", "arbitrary"),
            vmem_limit_bytes=V7X_VMEM_LIMIT_BYTES),
        name="mixer_in_proj",
    )(xb, w, *side_casts)


def _attn_kernel(sink_ref, q_ref, kp_ref, kc_ref, kn_ref, vp_ref, vc_ref, vn_ref, bias_ref,
                 o_ref, kcat_ref, vcat_ref):
    hk = pl.program_id(0)
    i = pl.program_id(1)
    tq = q_ref.shape[0]
    blocks_per_tile = tq // BLOCK
    n_blocks = pl.num_programs(1) * blocks_per_tile

    kcat_ref[0:BLOCK, :] = kp_ref[...]
    kcat_ref[BLOCK:BLOCK + tq, :] = kc_ref[...]
    kcat_ref[BLOCK + tq:, :] = kn_ref[...]
    vcat_ref[0:BLOCK, 0:HEAD_DIM] = vp_ref[...]
    vcat_ref[BLOCK:BLOCK + tq, 0:HEAD_DIM] = vc_ref[...]
    vcat_ref[BLOCK + tq:, 0:HEAD_DIM] = vn_ref[...]
    vcat_ref[:, HEAD_DIM:] = jnp.ones((tq + 2 * BLOCK, HEAD_DIM), BF16)

    qk_scale = LOG2E / math.sqrt(HEAD_DIM)

    def body(b, carry):
        row = pl.multiple_of(b * BLOCK, BLOCK)
        gb = i * blocks_per_tile + b
        var = jnp.where(gb == 0, 0, jnp.where(gb == n_blocks - 1, 2, 1))
        kw = kcat_ref[pl.ds(row, 3 * BLOCK), :]
        vw = vcat_ref[pl.ds(row, 3 * BLOCK), :]
        for g in range(Q_PER_KV):
            cols = slice(g * HEAD_DIM, (g + 1) * HEAD_DIM)
            sink = sink_ref[hk * Q_PER_KV + g]
            q = q_ref[pl.ds(row, BLOCK), cols]
            s = lax.dot_general(q, kw, (((1,), (1,)), ((), ())), preferred_element_type=F32)
            s = s * qk_scale + bias_ref[0, var, g]
            m = jnp.maximum(jnp.max(s, axis=-1, keepdims=True), sink)
            m_b = jnp.broadcast_to(m, (BLOCK, HEAD_DIM))
            p = jnp.exp2(s - jnp.concatenate([m_b] * 3, axis=-1))
            ov = jnp.dot(p.astype(BF16), vw, preferred_element_type=F32)
            denom = ov[:, HEAD_DIM:] + jnp.exp2(sink - m_b)
            o_ref[pl.ds(row, BLOCK), cols] = (ov[:, :HEAD_DIM] / denom).astype(o_ref.dtype)
        return carry

    lax.fori_loop(0, blocks_per_tile, body, 0, unroll=True)


def _attention(h, bias, sink_log2):
    s = h.shape[0]
    tq = ATTN_TQ
    r = tq // BLOCK
    nb = s // BLOCK
    assert nb >= 2
    gw = Q_PER_KV * HEAD_DIM

    def prev_map(col):
        return lambda hk, i: (jnp.maximum(i * r - 1, 0), col + hk)

    def cur_map(col):
        return lambda hk, i: (i, col + hk)

    def next_map(col):
        return lambda hk, i: (jnp.minimum((i + 1) * r, nb - 1), col + hk)

    return pl.pallas_call(
        _attn_kernel,
        grid=(N_KV_HEADS, s // tq),
        in_specs=[
            pl.BlockSpec(memory_space=pltpu.SMEM),
            pl.BlockSpec((tq, gw), lambda hk, i: (i, hk)),
            pl.BlockSpec((BLOCK, HEAD_DIM), prev_map(K_COL)),
            pl.BlockSpec((tq, HEAD_DIM), cur_map(K_COL)),
            pl.BlockSpec((BLOCK, HEAD_DIM), next_map(K_COL)),
            pl.BlockSpec((BLOCK, HEAD_DIM), prev_map(V_COL)),
            pl.BlockSpec((tq, HEAD_DIM), cur_map(V_COL)),
            pl.BlockSpec((BLOCK, HEAD_DIM), next_map(V_COL)),
            pl.BlockSpec((1, 3, Q_PER_KV, BLOCK, 3 * BLOCK), lambda hk, i: (hk, 0, 0, 0, 0)),
        ],
        out_specs=pl.BlockSpec((tq, gw), lambda hk, i: (i, hk)),
        out_shape=jax.ShapeDtypeStruct((s, ATTN_WIDTH), BF16),
        scratch_shapes=[
            pltpu.VMEM((tq + 2 * BLOCK, HEAD_DIM), BF16),
            pltpu.VMEM((tq + 2 * BLOCK, 2 * HEAD_DIM), BF16),
        ],
        compiler_params=pltpu.CompilerParams(
            dimension_semantics=("parallel", "parallel"),
            vmem_limit_bytes=V7X_VMEM_LIMIT_BYTES),
        name="banded_attention",
    )(sink_log2, h, h, h, h, h, h, h, bias)


def _attn_bias_table():
    a = np.arange(BLOCK)[:, None]
    c = np.arange(3 * BLOCK)[None, :]
    dist = np.abs(a + BLOCK - c)
    slopes = np.exp2(-8.0 * np.arange(1, N_Q_HEADS + 1, dtype=np.float32) / N_Q_HEADS)
    bias = (-slopes[:, None, None] * dist.astype(np.float32)[None]
            * np.float32(LOG2E))
    in_window = dist <= WINDOW
    variants = np.stack([in_window & (c >= BLOCK), in_window, in_window & (c < 2 * BLOCK)])
    table = np.where(variants[None], bias[:, None], np.float32(NEG_INF))
    table = table.reshape(N_KV_HEADS, Q_PER_KV, 3, BLOCK, 3 * BLOCK)
    return np.ascontiguousarray(table.transpose(0, 2, 1, 3, 4), dtype=np.float32)


def _pool_band_matrices(tm):
    t = np.arange(tm)[:, None]
    col = np.arange(tm + 2 * POOL_HALO)[None, :]
    pos = np.where(col < tm + POOL_HALO, col, col - tm - 2 * POOL_HALO)
    mats = []
    for w in POOL_WINDOWS:
        lo = t - w // 2
        hi = t + w - w // 2
        mats.append(((pos >= lo) & (pos < hi)).astype(np.float32))
    return np.stack(mats)


def _mixer_out_kernel(o_ref, pp_ref, pc_ref, pn_ref, ga_ref, gb_ref, x_ref, band_ref, wgrp_ref,
                      pscale_ref, wpa_ref, wpp_ref, wout_ref, g_ref, b_ref,
                      out_ref, pcat_ref):
    i = pl.program_id(0)
    tm = pc_ref.shape[0]
    seq = pl.num_programs(0) * tm

    zeros_halo = jnp.zeros((POOL_HALO, POOL_WIDTH), BF16)
    pcat_ref[0:tm, :] = pc_ref[...]
    pcat_ref[tm:tm + POOL_HALO, :] = jnp.where(i == pl.num_programs(0) - 1, zeros_halo, pn_ref[...])
    pcat_ref[tm + POOL_HALO:, :] = jnp.where(i == 0, zeros_halo, pp_ref[...])

    t = i * tm + lax.broadcasted_iota(jnp.int32, (tm, 1), 0)
    mixed = []
    for g, w in enumerate(POOL_WINDOWS):
        cols = slice(g * POOL_GROUP_DIM, (g + 1) * POOL_GROUP_DIM)
        pg = pcat_ref[:, cols]
        win_sum = jnp.dot(band_ref[g], pg, preferred_element_type=F32)
        lo = jnp.clip(t - w // 2, 0, seq)
        hi = jnp.clip(t + w - w // 2, 0, seq)
        mean = win_sum / (hi - lo).astype(F32)
        d = (mean - pc_ref[:, cols].astype(F32)).astype(BF16)
        mg = jnp.dot(d, wgrp_ref[g], preferred_element_type=F32) * pscale_ref[:, cols]
        mixed.append(mg.astype(BF16))
    mixed = jnp.concatenate(mixed, axis=-1)

    y_a = jnp.dot(o_ref[...], wpa_ref[...], preferred_element_type=F32)
    y_b = jnp.dot(mixed, wpp_ref[...], preferred_element_type=F32)
    merged = _sigmoid(ga_ref[...].astype(F32)) * y_a + _sigmoid(gb_ref[...].astype(F32)) * y_b
    z = jnp.dot(merged.astype(BF16), wout_ref[...], preferred_element_type=F32)
    out_ref[...] = _layer_norm(ALPHA * x_ref[...] + z, g_ref[...], b_ref[...])


def _mixer_out(o, h, x, band, wgrp, pscale, wpa, wpp, wout, g, b):
    s, d = x.shape
    tm = MIX_TM
    hr = tm // POOL_HALO
    n_halo = s // POOL_HALO
    pool_cb = POOL_OFF // POOL_WIDTH
    ga_cb = GATE_A_OFF // D_MODEL
    gb_cb = GATE_B_OFF // D_MODEL
    assert POOL_OFF % POOL_WIDTH == 0 and GATE_A_OFF % D_MODEL == 0 and GATE_B_OFF % D_MODEL == 0

    def const(shape):
        return pl.BlockSpec(shape, lambda i: (0,) * len(shape), pipeline_mode=pl.Buffered(1))

    return pl.pallas_call(
        _mixer_out_kernel,
        grid=(s // tm,),
        in_specs=[
            pl.BlockSpec((tm, ATTN_WIDTH), lambda i: (i, 0)),
            pl.BlockSpec((POOL_HALO, POOL_WIDTH), lambda i: (jnp.maximum(i * hr - 1, 0), pool_cb)),
            pl.BlockSpec((tm, POOL_WIDTH), lambda i: (i, pool_cb)),
            pl.BlockSpec((POOL_HALO, POOL_WIDTH),
                         lambda i: (jnp.minimum((i + 1) * hr, n_halo - 1), pool_cb)),
            pl.BlockSpec((tm, D_MODEL), lambda i: (i, ga_cb)),
            pl.BlockSpec((tm, D_MODEL), lambda i: (i, gb_cb)),
            pl.BlockSpec((tm, d), lambda i: (i, 0)),
            const(band.shape),
            const(wgrp.shape),
            const(pscale.shape),
            const(wpa.shape),
            const(wpp.shape),
            const(wout.shape),
            const(g.shape),
            const(b.shape),
        ],
        out_specs=pl.BlockSpec((tm, d), lambda i: (i, 0)),
        out_shape=jax.ShapeDtypeStruct((s, d), F32),
        scratch_shapes=[pltpu.VMEM((tm + 2 * POOL_HALO, POOL_WIDTH), BF16)],
        compiler_params=pltpu.CompilerParams(
            dimension_semantics=("parallel",),
            vmem_limit_bytes=V7X_VMEM_LIMIT_BYTES),
        name="mixer_out_ln",
    )(o, h, h, h, h, h, x, band, wgrp, pscale, wpa, wpp, wout, g, b)


def kernel(x, ffn1_w_gate, ffn1_w_up, ffn1_w_down, ln1_g, ln1_b, w_in, attn_sink, pool_w_groups,
           pool_scale, w_proj_attn, w_proj_pool, w_out, ln2_g, ln2_b, ffn2_w_gate, ffn2_w_up,
           ffn2_w_down, ln3_g, ln3_b):
    batch, seq, d = x.shape
    assert (batch, seq, d) == (1, SEQ, D_MODEL)
    x = x.reshape(seq, d)

    bias = jnp.asarray(_attn_bias_table())
    band = jnp.asarray(_pool_band_matrices(MIX_TM), BF16)
    depth = ffn1_w_gate.shape[0]
    for l in range(depth):
        x, xb, w_in_b = _ffn_ln(
            x, ffn1_w_gate[l].astype(BF16), ffn1_w_up[l].astype(BF16),
            ffn1_w_down[l].astype(BF16), ln1_g[l][None], ln1_b[l][None], "ffn1_ln1", True,
            side_casts=(w_in[l],))
        h, wg2, wu2, wd2, wpa, wpp, wout = _in_proj(
            xb, w_in_b, side_casts=(ffn2_w_gate[l], ffn2_w_up[l], ffn2_w_down[l],
                                    w_proj_attn[l], w_proj_pool[l], w_out[l]))
        o = _attention(h, bias, attn_sink[l].astype(F32) * LOG2E)
        x = _mixer_out(o, h, x, band, pool_w_groups[l].astype(BF16), pool_scale[l][None],
                       wpa, wpp, wout, ln2_g[l][None], ln2_b[l][None])
        (x,) = _ffn_ln(x, wg2, wu2, wd2, ln3_g[l][None], ln3_b[l][None], "ffn2_ln3", False)
    return x.reshape(batch, seq, d)
```

```python
import functools
import math

import numpy as np
import jax
import jax.numpy as jnp
from jax import lax
from jax.experimental import pallas as pl
from jax.experimental.pallas import tpu as pltpu

D_MODEL = 2048
SEQ = 16384
HEAD_DIM = 128
N_Q_HEADS = 16
N_KV_HEADS = 4
Q_PER_KV = N_Q_HEADS // N_KV_HEADS
ATTN_WIDTH = N_Q_HEADS * HEAD_DIM
KV_WIDTH = N_KV_HEADS * HEAD_DIM
WINDOW = 128
BLOCK = 128
POOL_WINDOWS = (2, 4, 8, 16)
N_POOL_GROUPS = len(POOL_WINDOWS)
POOL_GROUP_DIM = 256
POOL_WIDTH = N_POOL_GROUPS * POOL_GROUP_DIM
D_FF = 5632
ALPHA = 2.0 ** 0.25
LN_EPS = 1e-5
NEG_INF = -1e30
LOG2E = math.log2(math.e)
IN_WIDTH = ATTN_WIDTH + 2 * KV_WIDTH + POOL_WIDTH + 2 * D_MODEL

Q_COL = 0
K_COL = ATTN_WIDTH // HEAD_DIM
V_COL = (ATTN_WIDTH + KV_WIDTH) // HEAD_DIM
POOL_OFF = ATTN_WIDTH + 2 * KV_WIDTH
GATE_A_OFF = POOL_OFF + POOL_WIDTH
GATE_B_OFF = GATE_A_OFF + D_MODEL

V7X_VMEM_LIMIT_BYTES = 56 * 1024 * 1024
POOL_HALO = 16

FFN_TM = 512
FFN_TF = 512
FFN_NORM_STEPS = 8
PROJ_TM = 2048
PROJ_TN = 1024
ATTN_TQ = 1024
MIX_TM = 256
MIX_NORM_SLABS = 4

F32 = jnp.float32
BF16 = jnp.bfloat16


def _layer_norm(y, g, b):
    mu = jnp.mean(y, axis=-1, keepdims=True)
    yc = y - mu
    var = jnp.mean(yc * yc, axis=-1, keepdims=True)
    return yc * lax.rsqrt(var + LN_EPS) * g + b


def _sigmoid(x):
    return 1.0 / (1.0 + jnp.exp(-x))


def _ffn_ln_kernel(n_out, n_side, x_ref, wg_ref, wu_ref, wd_ref, g_ref, b_ref, *rest):
    side_in = rest[:n_side]
    out_refs = rest[n_side:n_side + n_out]
    side_out = rest[n_side + n_out:2 * n_side + n_out]
    xb_ref, acc_ref, ypre_ref = rest[2 * n_side + n_out:]
    i = pl.program_id(0)
    j = pl.program_id(1)
    n_tiles = pl.num_programs(0) - 1
    last_j = pl.num_programs(1) - 1
    slab = x_ref.shape[0] // FFN_NORM_STEPS

    def down_proj(xb):
        gate = jnp.dot(xb, wg_ref[...], preferred_element_type=F32)
        up = jnp.dot(xb, wu_ref[...], preferred_element_type=F32)
        act = (gate * _sigmoid(gate) * up).astype(BF16)
        return jnp.dot(act, wd_ref[...], preferred_element_type=F32)

    def norm_slab():
        rows = pl.ds(pl.multiple_of(j * slab, slab), slab)
        y = _layer_norm(ypre_ref[rows, :], g_ref[...], b_ref[...])
        out_refs[0][rows, :] = y
        if n_out == 2:
            out_refs[1][rows, :] = y.astype(BF16)

    def first_chunk(with_norm):
        if with_norm:
            norm_slab()
        xb = x_ref[...].astype(BF16)
        xb_ref[...] = xb
        acc_ref[...] = down_proj(xb)

    def middle_chunk(with_norm):
        if with_norm:
            norm_slab()
        acc_ref[...] += down_proj(xb_ref[...])

    real_tile = i < n_tiles
    norm_due = (i > 0) & (j < FFN_NORM_STEPS)
    middle = (j > 0) & (j < last_j)
    pl.when((j == 0) & (i == 0))(lambda: first_chunk(False))
    pl.when((j == 0) & (i > 0) & real_tile)(lambda: first_chunk(True))
    pl.when(middle & real_tile & norm_due)(lambda: middle_chunk(True))
    pl.when(middle & real_tile & jnp.logical_not(norm_due))(lambda: middle_chunk(False))
    pl.when(jnp.logical_not(real_tile) & norm_due)(norm_slab)

    @pl.when((j == last_j) & real_tile)
    def _():
        ypre_ref[...] = ALPHA * x_ref[...] + 0.5 * (acc_ref[...] + down_proj(xb_ref[...]))
        for src, dst in zip(side_in, side_out):
            dst[...] = src[...].astype(BF16)


def _ffn_ln(x, wg, wu, wd, g, b, name, with_bf16_copy, side_casts=()):
    s, d = x.shape
    f = wg.shape[1]
    tm, tf = FFN_TM, FFN_TF
    n, nj = s // tm, f // tf
    n_out = 2 if with_bf16_copy else 1
    assert FFN_NORM_STEPS < nj and tm % (16 * FFN_NORM_STEPS) == 0

    def wcol(i, j):
        return jnp.where(i < n, j, nj - 1)

    side_specs, side_shapes = [], []
    for w in side_casts:
        rows = w.shape[0] // n
        assert rows * n == w.shape[0] and rows % 16 == 0
        side_specs.append(pl.BlockSpec((rows, w.shape[1]), lambda i, j: (jnp.minimum(i, n - 1), 0)))
        side_shapes.append(jax.ShapeDtypeStruct(w.shape, BF16))
    out_spec = pl.BlockSpec((tm, d), lambda i, j: (jnp.maximum(i - 1, 0), 0))
    return pl.pallas_call(
        functools.partial(_ffn_ln_kernel, n_out, len(side_casts)),
        grid=(n + 1, nj),
        in_specs=[
            pl.BlockSpec((tm, d), lambda i, j: (jnp.minimum(i, n - 1), 0)),
            pl.BlockSpec((d, tf), lambda i, j: (0, wcol(i, j))),
            pl.BlockSpec((d, tf), lambda i, j: (0, wcol(i, j))),
            pl.BlockSpec((tf, d), lambda i, j: (wcol(i, j), 0)),
            pl.BlockSpec((1, d), lambda i, j: (0, 0)),
            pl.BlockSpec((1, d), lambda i, j: (0, 0)),
        ] + side_specs,
        out_specs=[out_spec] * n_out + side_specs,
        out_shape=[jax.ShapeDtypeStruct((s, d), F32),
                   jax.ShapeDtypeStruct((s, d), BF16)][:n_out] + side_shapes,
        scratch_shapes=[pltpu.VMEM((tm, d), BF16), pltpu.VMEM((tm, d), F32),
                        pltpu.VMEM((tm, d), F32)],
        compiler_params=pltpu.CompilerParams(
            dimension_semantics=("arbitrary", "arbitrary"),
            vmem_limit_bytes=V7X_VMEM_LIMIT_BYTES),
        name=name,
    )(x, wg, wu, wd, g, b, *side_casts)


def _proj_kernel(repeats, x_ref, w_ref, *rest):
    n_side = len(repeats)
    side_in = rest[:n_side]
    o_ref = rest[n_side]
    side_out = rest[n_side + 1:]
    step = pl.program_id(0) * pl.num_programs(1) + pl.program_id(1)

    o_ref[...] = jnp.dot(x_ref[...], w_ref[...], preferred_element_type=F32).astype(o_ref.dtype)

    for src, dst, rep in zip(side_in, side_out, repeats):
        if rep == 1:
            dst[...] = src[...].astype(BF16)
        else:
            @pl.when(step % rep == 0)
            def _(src=src, dst=dst):
                dst[...] = src[...].astype(BF16)


def _in_proj(xb, w, side_casts):
    s, d = xb.shape
    n = w.shape[1]
    tm, tn = PROJ_TM, PROJ_TN
    gi, gj = s // tm, n // tn
    steps = gi * gj

    def step_map(rep):
        return lambda i, j: ((i * gj + j) // rep, 0)

    side_specs, side_shapes, repeats = [], [], []
    for a in side_casts:
        n_blocks = steps
        while a.shape[0] % (16 * n_blocks):
            n_blocks //= 2
        assert n_blocks >= 1 and steps % n_blocks == 0
        repeats.append(steps // n_blocks)
        side_specs.append(pl.BlockSpec((a.shape[0] // n_blocks, a.shape[1]),
                                       step_map(repeats[-1])))
        side_shapes.append(jax.ShapeDtypeStruct(a.shape, BF16))
    return pl.pallas_call(
        functools.partial(_proj_kernel, tuple(repeats)),
        grid=(gi, gj),
        in_specs=[
            pl.BlockSpec((tm, d), lambda i, j: (i, 0)),
            pl.BlockSpec((d, tn), lambda i, j: (0, j)),
        ] + side_specs,
        out_specs=[pl.BlockSpec((tm, tn), lambda i, j: (i, j))] + side_specs,
        out_shape=[jax.ShapeDtypeStruct((s, n), BF16)] + side_shapes,
        compiler_params=pltpu.CompilerParams(
            dimension_semantics=("arbitrary", "arbitrary"),
            vmem_limit_bytes=V7X_VMEM_LIMIT_BYTES),
        name="mixer_in_proj",
    )(xb, w, *side_casts)


def _attn_kernel(sink_ref, q_ref, kp_ref, kc_ref, kn_ref, vp_ref, vc_ref, vn_ref, bias_ref,
                 o_ref, kcat_ref, vcat_ref):
    hk = pl.program_id(0)
    i = pl.program_id(1)
    tq = q_ref.shape[0]
    blocks_per_tile = tq // BLOCK
    n_blocks = pl.num_programs(1) * blocks_per_tile

    kcat_ref[0:BLOCK, :] = kp_ref[...]
    kcat_ref[BLOCK:BLOCK + tq, :] = kc_ref[...]
    kcat_ref[BLOCK + tq:, :] = kn_ref[...]
    vcat_ref[0:BLOCK, 0:HEAD_DIM] = vp_ref[...]
    vcat_ref[BLOCK:BLOCK + tq, 0:HEAD_DIM] = vc_ref[...]
    vcat_ref[BLOCK + tq:, 0:HEAD_DIM] = vn_ref[...]
    vcat_ref[:, HEAD_DIM:] = jnp.ones((tq + 2 * BLOCK, HEAD_DIM), BF16)

    qk_scale = LOG2E / math.sqrt(HEAD_DIM)

    def bias_variant(b):
        if b == 0:
            return jnp.where(i == 0, 0, 1)
        if b == blocks_per_tile - 1:
            return jnp.where(i == pl.num_programs(1) - 1, 2, 1)
        return 1

    def scores(b, g):
        q = q_ref[b * BLOCK:(b + 1) * BLOCK, g * HEAD_DIM:(g + 1) * HEAD_DIM]
        kw = kcat_ref[b * BLOCK:(b + 3) * BLOCK, :]
        return lax.dot_general(q, kw, (((1,), (1,)), ((), ())), preferred_element_type=F32)

    def finish(b, g, s):
        sink = sink_ref[hk * Q_PER_KV + g]
        s = s * qk_scale + bias_ref[0, bias_variant(b), g]
        m = jnp.maximum(jnp.max(s, axis=-1, keepdims=True), sink)
        m_b = jnp.broadcast_to(m, (BLOCK, HEAD_DIM))
        p = jnp.exp2(s - jnp.concatenate([m_b] * 3, axis=-1))
        vw = vcat_ref[b * BLOCK:(b + 3) * BLOCK, :]
        ov = jnp.dot(p.astype(BF16), vw, preferred_element_type=F32)
        denom = ov[:, HEAD_DIM:] + jnp.exp2(sink - m_b)
        o_ref[b * BLOCK:(b + 1) * BLOCK, g * HEAD_DIM:(g + 1) * HEAD_DIM] = (
            ov[:, :HEAD_DIM] / denom).astype(o_ref.dtype)

    units = [(b, g) for b in range(blocks_per_tile) for g in range(Q_PER_KV)]
    s_next = scores(*units[0])
    for u, unit in enumerate(units):
        s_cur = s_next
        if u + 1 < len(units):
            s_next = scores(*units[u + 1])
        finish(*unit, s_cur)


def _attention(h, bias, sink_log2):
    s = h.shape[0]
    tq = ATTN_TQ
    r = tq // BLOCK
    nb = s // BLOCK
    assert nb >= 2
    gw = Q_PER_KV * HEAD_DIM

    def prev_map(col):
        return lambda hk, i: (jnp.maximum(i * r - 1, 0), col + hk)

    def cur_map(col):
        return lambda hk, i: (i, col + hk)

    def next_map(col):
        return lambda hk, i: (jnp.minimum((i + 1) * r, nb - 1), col + hk)

    return pl.pallas_call(
        _attn_kernel,
        grid=(N_KV_HEADS, s // tq),
        in_specs=[
            pl.BlockSpec(memory_space=pltpu.SMEM),
            pl.BlockSpec((tq, gw), lambda hk, i: (i, hk)),
            pl.BlockSpec((BLOCK, HEAD_DIM), prev_map(K_COL)),
            pl.BlockSpec((tq, HEAD_DIM), cur_map(K_COL)),
            pl.BlockSpec((BLOCK, HEAD_DIM), next_map(K_COL)),
            pl.BlockSpec((BLOCK, HEAD_DIM), prev_map(V_COL)),
            pl.BlockSpec((tq, HEAD_DIM), cur_map(V_COL)),
            pl.BlockSpec((BLOCK, HEAD_DIM), next_map(V_COL)),
            pl.BlockSpec((1, 3, Q_PER_KV, BLOCK, 3 * BLOCK), lambda hk, i: (hk, 0, 0, 0, 0)),
        ],
        out_specs=pl.BlockSpec((tq, gw), lambda hk, i: (i, hk)),
        out_shape=jax.ShapeDtypeStruct((s, ATTN_WIDTH), BF16),
        scratch_shapes=[
            pltpu.VMEM((tq + 2 * BLOCK, HEAD_DIM), BF16),
            pltpu.VMEM((tq + 2 * BLOCK, 2 * HEAD_DIM), BF16),
        ],
        compiler_params=pltpu.CompilerParams(
            dimension_semantics=("parallel", "parallel"),
            vmem_limit_bytes=V7X_VMEM_LIMIT_BYTES),
        name="banded_attention",
    )(sink_log2, h, h, h, h, h, h, h, bias)


def _attn_bias_table():
    a = np.arange(BLOCK)[:, None]
    c = np.arange(3 * BLOCK)[None, :]
    dist = np.abs(a + BLOCK - c)
    slopes = np.exp2(-8.0 * np.arange(1, N_Q_HEADS + 1, dtype=np.float32) / N_Q_HEADS)
    bias = (-slopes[:, None, None] * dist.astype(np.float32)[None]
            * np.float32(LOG2E))
    in_window = dist <= WINDOW
    variants = np.stack([in_window & (c >= BLOCK), in_window, in_window & (c < 2 * BLOCK)])
    table = np.where(variants[None], bias[:, None], np.float32(NEG_INF))
    table = table.reshape(N_KV_HEADS, Q_PER_KV, 3, BLOCK, 3 * BLOCK)
    return np.ascontiguousarray(table.transpose(0, 2, 1, 3, 4), dtype=np.float32)


def _pool_band_matrices(tm):
    t = np.arange(tm)[:, None]
    col = np.arange(tm + 2 * POOL_HALO)[None, :]
    pos = np.where(col < tm + POOL_HALO, col, col - tm - 2 * POOL_HALO)
    mats = []
    for w in POOL_WINDOWS:
        lo = t - w // 2
        hi = t + w - w // 2
        mats.append(((pos >= lo) & (pos < hi)).astype(np.float32))
    return np.stack(mats)


def _mixer_out_kernel(o_ref, pp_ref, pc_ref, pn_ref, ga_ref, gb_ref, x_ref, band_ref, wgrp_ref,
                      pscale_ref, wpa_ref, wpp_ref, wout_ref, g_ref, b_ref,
                      out_ref, pcat_ref, zpre_ref):
    i = pl.program_id(0)
    n_tiles = pl.num_programs(0) - 1
    tm = pc_ref.shape[0]
    seq = n_tiles * tm
    slab = tm // MIX_NORM_SLABS

    def norm_slab(k):
        rows = slice(k * slab, (k + 1) * slab)
        out_ref[rows, :] = _layer_norm(zpre_ref[rows, :], g_ref[...], b_ref[...])

    def mix_tile(with_norm):
        zeros_halo = jnp.zeros((POOL_HALO, POOL_WIDTH), BF16)
        pcat_ref[0:tm, :] = pc_ref[...]
        pcat_ref[tm:tm + POOL_HALO, :] = jnp.where(i == n_tiles - 1, zeros_halo, pn_ref[...])
        pcat_ref[tm + POOL_HALO:, :] = jnp.where(i == 0, zeros_halo, pp_ref[...])

        if with_norm:
            norm_slab(0)
        y_a = jnp.dot(o_ref[...], wpa_ref[...], preferred_element_type=F32)
        if with_norm:
            norm_slab(1)

        t = i * tm + lax.broadcasted_iota(jnp.int32, (tm, 1), 0)
        groups = range(N_POOL_GROUPS)
        cols = [slice(g * POOL_GROUP_DIM, (g + 1) * POOL_GROUP_DIM) for g in groups]
        win_sums = [jnp.dot(band_ref[g], pcat_ref[:, cols[g]], preferred_element_type=F32)
                    for g in groups]
        diffs = []
        for g, w in enumerate(POOL_WINDOWS):
            lo = jnp.clip(t - w // 2, 0, seq)
            hi = jnp.clip(t + w - w // 2, 0, seq)
            mean = win_sums[g] / (hi - lo).astype(F32)
            diffs.append((mean - pc_ref[:, cols[g]].astype(F32)).astype(BF16))
        maps = [jnp.dot(diffs[g], wgrp_ref[g], preferred_element_type=F32) for g in groups]
        mixed = jnp.concatenate(
            [(maps[g] * pscale_ref[:, cols[g]]).astype(BF16) for g in groups], axis=-1)

        y_b = jnp.dot(mixed, wpp_ref[...], preferred_element_type=F32)
        merged = (_sigmoid(ga_ref[...].astype(F32)) * y_a
                  + _sigmoid(gb_ref[...].astype(F32)) * y_b)
        if with_norm:
            norm_slab(2)
        z = jnp.dot(merged.astype(BF16), wout_ref[...], preferred_element_type=F32)
        if with_norm:
            norm_slab(3)
        zpre_ref[...] = ALPHA * x_ref[...] + z

    pl.when(i == 0)(lambda: mix_tile(False))
    pl.when((i > 0) & (i < n_tiles))(lambda: mix_tile(True))

    @pl.when(i == n_tiles)
    def _():
        for k in range(MIX_NORM_SLABS):
            norm_slab(k)


def _mixer_out(o, h, x, band, wgrp, pscale, wpa, wpp, wout, g, b):
    s, d = x.shape
    tm = MIX_TM
    hr = tm // POOL_HALO
    n_halo = s // POOL_HALO
    pool_cb = POOL_OFF // POOL_WIDTH
    ga_cb = GATE_A_OFF // D_MODEL
    gb_cb = GATE_B_OFF // D_MODEL
    assert POOL_OFF % POOL_WIDTH == 0 and GATE_A_OFF % D_MODEL == 0 and GATE_B_OFF % D_MODEL == 0

    n = s // tm
    assert tm % (8 * MIX_NORM_SLABS) == 0

    def const(shape):
        return pl.BlockSpec(shape, lambda i: (0,) * len(shape), pipeline_mode=pl.Buffered(1))

    def tile(i):
        return jnp.minimum(i, n - 1)

    return pl.pallas_call(
        _mixer_out_kernel,
        grid=(n + 1,),
        in_specs=[
            pl.BlockSpec((tm, ATTN_WIDTH), lambda i: (tile(i), 0)),
            pl.BlockSpec((POOL_HALO, POOL_WIDTH),
                         lambda i: (jnp.maximum(tile(i) * hr - 1, 0), pool_cb)),
            pl.BlockSpec((tm, POOL_WIDTH), lambda i: (tile(i), pool_cb)),
            pl.BlockSpec((POOL_HALO, POOL_WIDTH),
                         lambda i: (jnp.minimum((tile(i) + 1) * hr, n_halo - 1), pool_cb)),
            pl.BlockSpec((tm, D_MODEL), lambda i: (tile(i), ga_cb)),
            pl.BlockSpec((tm, D_MODEL), lambda i: (tile(i), gb_cb)),
            pl.BlockSpec((tm, d), lambda i: (tile(i), 0)),
            const(band.shape),
            const(wgrp.shape),
            const(pscale.shape),
            const(wpa.shape),
            const(wpp.shape),
            const(wout.shape),
            const(g.shape),
            const(b.shape),
        ],
        out_specs=pl.BlockSpec((tm, d), lambda i: (jnp.maximum(i - 1, 0), 0)),
        out_shape=jax.ShapeDtypeStruct((s, d), F32),
        scratch_shapes=[pltpu.VMEM((tm + 2 * POOL_HALO, POOL_WIDTH), BF16),
                        pltpu.VMEM((tm, d), F32)],
        compiler_params=pltpu.CompilerParams(
            dimension_semantics=("arbitrary",),
            vmem_limit_bytes=V7X_VMEM_LIMIT_BYTES),
        name="mixer_out_ln",
    )(o, h, h, h, h, h, x, band, wgrp, pscale, wpa, wpp, wout, g, b)


def kernel(x, ffn1_w_gate, ffn1_w_up, ffn1_w_down, ln1_g, ln1_b, w_in, attn_sink, pool_w_groups,
           pool_scale, w_proj_attn, w_proj_pool, w_out, ln2_g, ln2_b, ffn2_w_gate, ffn2_w_up,
           ffn2_w_down, ln3_g, ln3_b):
    batch, seq, d = x.shape
    assert (batch, seq, d) == (1, SEQ, D_MODEL)
    x = x.reshape(seq, d)

    bias = jnp.asarray(_attn_bias_table())
    band = jnp.asarray(_pool_band_matrices(MIX_TM), BF16)
    depth = ffn1_w_gate.shape[0]
    for l in range(depth):
        x, xb, w_in_b = _ffn_ln(
            x, ffn1_w_gate[l].astype(BF16), ffn1_w_up[l].astype(BF16),
            ffn1_w_down[l].astype(BF16), ln1_g[l][None], ln1_b[l][None], "ffn1_ln1", True,
            side_casts=(w_in[l],))
        h, wg2, wu2, wd2, wpa, wpp, wout = _in_proj(
            xb, w_in_b, side_casts=(ffn2_w_gate[l], ffn2_w_up[l], ffn2_w_down[l],
                                    w_proj_attn[l], w_proj_pool[l], w_out[l]))
        o = _attention(h, bias, attn_sink[l].astype(F32) * LOG2E)
        x = _mixer_out(o, h, x, band, pool_w_groups[l].astype(BF16), pool_scale[l][None],
                       wpa, wpp, wout, ln2_g[l][None], ln2_b[l][None])
        (x,) = _ffn_ln(x, wg2, wu2, wd2, ln3_g[l][None], ln3_b[l][None], "ffn2_ln3", False)
    return x.reshape(batch, seq, d)
```

```python
import functools
import math

import numpy as np
import jax
import jax.numpy as jnp
from jax import lax
from jax.experimental import pallas as pl
from jax.experimental.pallas import tpu as pltpu

D_MODEL = 2048
SEQ = 16384
HEAD_DIM = 128
N_Q_HEADS = 16
N_KV_HEADS = 4
Q_PER_KV = N_Q_HEADS // N_KV_HEADS
ATTN_WIDTH = N_Q_HEADS * HEAD_DIM
KV_WIDTH = N_KV_HEADS * HEAD_DIM
WINDOW = 128
BLOCK = 128
POOL_WINDOWS = (2, 4, 8, 16)
N_POOL_GROUPS = len(POOL_WINDOWS)
POOL_GROUP_DIM = 256
POOL_WIDTH = N_POOL_GROUPS * POOL_GROUP_DIM
D_FF = 5632
ALPHA = 2.0 ** 0.25
LN_EPS = 1e-5
NEG_INF = -1e30
LOG2E = math.log2(math.e)
IN_WIDTH = ATTN_WIDTH + 2 * KV_WIDTH + POOL_WIDTH + 2 * D_MODEL

Q_COL = 0
K_COL = ATTN_WIDTH // HEAD_DIM
V_COL = (ATTN_WIDTH + KV_WIDTH) // HEAD_DIM
POOL_OFF = ATTN_WIDTH + 2 * KV_WIDTH
GATE_A_OFF = POOL_OFF + POOL_WIDTH
GATE_B_OFF = GATE_A_OFF + D_MODEL

V7X_VMEM_LIMIT_BYTES = 56 * 1024 * 1024
POOL_HALO = 16

FFN_TM = 512
FFN_TF = 512
FFN_NORM_STEPS = 8
FFN_X_PIECES = 4
FFN_X_SWITCH = (2, 3, 6, 7)
PROJ_TM = 2048
PROJ_TN = 1024
ATTN_TQ = 1024
MIX_TM = 256
MIX_NORM_SLABS = 4

F32 = jnp.float32
BF16 = jnp.bfloat16


def _layer_norm(y, g, b):
    mu = jnp.mean(y, axis=-1, keepdims=True)
    yc = y - mu
    var = jnp.mean(yc * yc, axis=-1, keepdims=True)
    return yc * lax.rsqrt(var + LN_EPS) * g + b


def _sigmoid(x):
    return 1.0 / (1.0 + jnp.exp(-x))


def _ffn_ln_kernel(n_out, n_side, *refs):
    x_refs = refs[:FFN_X_PIECES]
    wg_ref, wu_ref, wd_ref, g_ref, b_ref = refs[FFN_X_PIECES:FFN_X_PIECES + 5]
    rest = refs[FFN_X_PIECES + 5:]
    side_in = rest[:n_side]
    out_refs = rest[n_side:n_side + n_out]
    side_out = rest[n_side + n_out:2 * n_side + n_out]
    xb_ref, acc_ref, ypre_ref = rest[2 * n_side + n_out:]
    i = pl.program_id(0)
    j = pl.program_id(1)
    n_tiles = pl.num_programs(0) - 1
    last_j = pl.num_programs(1) - 1
    slab = xb_ref.shape[0] // FFN_NORM_STEPS
    slabs_per_half = FFN_NORM_STEPS // 2

    def down_proj(xb):
        gate = jnp.dot(xb, wg_ref[...], preferred_element_type=F32)
        up = jnp.dot(xb, wu_ref[...], preferred_element_type=F32)
        act = (gate * _sigmoid(gate) * up).astype(BF16)
        return jnp.dot(act, wd_ref[...], preferred_element_type=F32)

    def norm_slab():
        rows = pl.ds(pl.multiple_of(j * slab, slab), slab)
        half_rows = pl.ds(pl.multiple_of((j % slabs_per_half) * slab, slab), slab)
        y = _layer_norm(ypre_ref[rows, :], g_ref[...], b_ref[...])
        out_refs[0][half_rows, :] = y
        if n_out == 2:
            out_refs[1][half_rows, :] = y.astype(BF16)

    def first_chunk(with_norm):
        if with_norm:
            norm_slab()
        x = jnp.concatenate([r[...] for r in x_refs], axis=0)
        xb = x.astype(BF16)
        xb_ref[...] = xb
        acc_ref[...] = (2.0 * ALPHA) * x + down_proj(xb)

    def middle_chunk(with_norm):
        if with_norm:
            norm_slab()
        acc_ref[...] += down_proj(xb_ref[...])

    real_tile = i < n_tiles
    norm_due = (i > 0) & (j < FFN_NORM_STEPS)
    middle = (j > 0) & (j < last_j)
    pl.when((j == 0) & (i == 0))(lambda: first_chunk(False))
    pl.when((j == 0) & (i > 0) & real_tile)(lambda: first_chunk(True))
    pl.when(middle & real_tile & norm_due)(lambda: middle_chunk(True))
    pl.when(middle & real_tile & jnp.logical_not(norm_due))(lambda: middle_chunk(False))
    pl.when(jnp.logical_not(real_tile) & norm_due)(norm_slab)

    @pl.when((j == last_j) & real_tile)
    def _():
        ypre_ref[...] = 0.5 * (acc_ref[...] + down_proj(xb_ref[...]))
        for src, dst in zip(side_in, side_out):
            dst[...] = src[...].astype(BF16)


def _ffn_ln(x, wg, wu, wd, g, b, name, with_bf16_copy, side_casts=()):
    s, d = x.shape
    f = wg.shape[1]
    tm, tf = FFN_TM, FFN_TF
    n, nj = s // tm, f // tf
    n_out = 2 if with_bf16_copy else 1
    assert FFN_NORM_STEPS < nj and tm % (16 * FFN_NORM_STEPS) == 0

    def wcol(i, j):
        return jnp.where(i < n, j, nj - 1)

    side_specs, side_shapes = [], []
    for w in side_casts:
        rows = w.shape[0] // n
        assert rows * n == w.shape[0] and rows % 16 == 0
        side_specs.append(pl.BlockSpec((rows, w.shape[1]), lambda i, j: (jnp.minimum(i, n - 1), 0)))
        side_shapes.append(jax.ShapeDtypeStruct(w.shape, BF16))
    piece = tm // FFN_X_PIECES
    assert piece * FFN_X_PIECES == tm and len(FFN_X_SWITCH) == FFN_X_PIECES
    assert 0 < min(FFN_X_SWITCH) and max(FFN_X_SWITCH) < nj

    def x_spec(p):
        def index_map(i, j):
            tile = jnp.minimum(i + (j >= FFN_X_SWITCH[p]).astype(jnp.int32), n - 1)
            return (tile * FFN_X_PIECES + p, 0)
        return pl.BlockSpec((piece, d), index_map)

    half_steps = FFN_NORM_STEPS // 2

    def out_index(i, j):
        done = (j >= half_steps).astype(jnp.int32) + (j >= 2 * half_steps).astype(jnp.int32)
        return (jnp.where(i == 0, 0, jnp.minimum(2 * (i - 1) + done, 2 * n - 1)), 0)

    out_spec = pl.BlockSpec((tm // 2, d), out_index)
    return pl.pallas_call(
        functools.partial(_ffn_ln_kernel, n_out, len(side_casts)),
        grid=(n + 1, nj),
        in_specs=[x_spec(p) for p in range(FFN_X_PIECES)] + [
            pl.BlockSpec((d, tf), lambda i, j: (0, wcol(i, j))),
            pl.BlockSpec((d, tf), lambda i, j: (0, wcol(i, j))),
            pl.BlockSpec((tf, d), lambda i, j: (wcol(i, j), 0)),
            pl.BlockSpec((1, d), lambda i, j: (0, 0)),
            pl.BlockSpec((1, d), lambda i, j: (0, 0)),
        ] + side_specs,
        out_specs=[out_spec] * n_out + side_specs,
        out_shape=[jax.ShapeDtypeStruct((s, d), F32),
                   jax.ShapeDtypeStruct((s, d), BF16)][:n_out] + side_shapes,
        scratch_shapes=[pltpu.VMEM((tm, d), BF16), pltpu.VMEM((tm, d), F32),
                        pltpu.VMEM((tm, d), F32)],
        compiler_params=pltpu.CompilerParams(
            dimension_semantics=("arbitrary", "arbitrary"),
            vmem_limit_bytes=V7X_VMEM_LIMIT_BYTES),
        name=name,
    )(*([x] * FFN_X_PIECES), wg, wu, wd, g, b, *side_casts)


def _proj_kernel(repeats, x_ref, w_ref, *rest):
    n_side = len(repeats)
    side_in = rest[:n_side]
    o_ref = rest[n_side]
    side_out = rest[n_side + 1:]
    step = pl.program_id(0) * pl.num_programs(1) + pl.program_id(1)

    o_ref[...] = jnp.dot(x_ref[...], w_ref[...], preferred_element_type=F32).astype(o_ref.dtype)

    for src, dst, rep in zip(side_in, side_out, repeats):
        if rep == 1:
            dst[...] = src[...].astype(BF16)
        else:
            @pl.when(step % rep == 0)
            def _(src=src, dst=dst):
                dst[...] = src[...].astype(BF16)


def _in_proj(xb, w, side_casts):
    s, d = xb.shape
    n = w.shape[1]
    tm, tn = PROJ_TM, PROJ_TN
    gi, gj = s // tm, n // tn
    steps = gi * gj

    def step_map(rep):
        return lambda i, j: ((i * gj + j) // rep, 0)

    side_specs, side_shapes, repeats = [], [], []
    for a in side_casts:
        n_blocks = steps
        while a.shape[0] % (16 * n_blocks):
            n_blocks //= 2
        assert n_blocks >= 1 and steps % n_blocks == 0
        repeats.append(steps // n_blocks)
        side_specs.append(pl.BlockSpec((a.shape[0] // n_blocks, a.shape[1]),
                                       step_map(repeats[-1])))
        side_shapes.append(jax.ShapeDtypeStruct(a.shape, BF16))
    return pl.pallas_call(
        functools.partial(_proj_kernel, tuple(repeats)),
        grid=(gi, gj),
        in_specs=[
            pl.BlockSpec((tm, d), lambda i, j: (i, 0)),
            pl.BlockSpec((d, tn), lambda i, j: (0, j)),
        ] + side_specs,
        out_specs=[pl.BlockSpec((tm, tn), lambda i, j: (i, j))] + side_specs,
        out_shape=[jax.ShapeDtypeStruct((s, n), BF16)] + side_shapes,
        compiler_params=pltpu.CompilerParams(
            dimension_semantics=("arbitrary", "arbitrary"),
            vmem_limit_bytes=V7X_VMEM_LIMIT_BYTES),
        name="mixer_in_proj",
    )(xb, w, *side_casts)


def _attn_kernel(sink_ref, q_ref, kp_ref, kc_ref, kn_ref, vp_ref, vc_ref, vn_ref, bias_ref,
                 o_ref, kcat_ref, vcat_ref):
    hk = pl.program_id(0)
    i = pl.program_id(1)
    tq = q_ref.shape[0]
    blocks_per_tile = tq // BLOCK
    n_blocks = pl.num_programs(1) * blocks_per_tile

    kcat_ref[0:BLOCK, :] = kp_ref[...]
    kcat_ref[BLOCK:BLOCK + tq, :] = kc_ref[...]
    kcat_ref[BLOCK + tq:, :] = kn_ref[...]
    vcat_ref[0:BLOCK, 0:HEAD_DIM] = vp_ref[...]
    vcat_ref[BLOCK:BLOCK + tq, 0:HEAD_DIM] = vc_ref[...]
    vcat_ref[BLOCK + tq:, 0:HEAD_DIM] = vn_ref[...]
    vcat_ref[:, HEAD_DIM:] = jnp.ones((tq + 2 * BLOCK, HEAD_DIM), BF16)

    qk_scale = LOG2E / math.sqrt(HEAD_DIM)

    def bias_variant(b):
        if b == 0:
            return jnp.where(i == 0, 0, 1)
        if b == blocks_per_tile - 1:
            return jnp.where(i == pl.num_programs(1) - 1, 2, 1)
        return 1

    def scores(b, g):
        q = q_ref[b * BLOCK:(b + 1) * BLOCK, g * HEAD_DIM:(g + 1) * HEAD_DIM]
        kw = kcat_ref[b * BLOCK:(b + 3) * BLOCK, :]
        return lax.dot_general(q, kw, (((1,), (1,)), ((), ())), preferred_element_type=F32)

    def finish(b, g, s):
        sink = sink_ref[hk * Q_PER_KV + g]
        s = s * qk_scale + bias_ref[0, bias_variant(b), g]
        m = jnp.maximum(jnp.max(s, axis=-1, keepdims=True), sink)
        m_b = jnp.broadcast_to(m, (BLOCK, HEAD_DIM))
        p = jnp.exp2(s - jnp.concatenate([m_b] * 3, axis=-1))
        vw = vcat_ref[b * BLOCK:(b + 3) * BLOCK, :]
        ov = jnp.dot(p.astype(BF16), vw, preferred_element_type=F32)
        denom = ov[:, HEAD_DIM:] + jnp.exp2(sink - m_b)
        o_ref[b * BLOCK:(b + 1) * BLOCK, g * HEAD_DIM:(g + 1) * HEAD_DIM] = (
            ov[:, :HEAD_DIM] / denom).astype(o_ref.dtype)

    units = [(b, g) for b in range(blocks_per_tile) for g in range(Q_PER_KV)]
    s_next = scores(*units[0])
    for u, unit in enumerate(units):
        s_cur = s_next
        if u + 1 < len(units):
            s_next = scores(*units[u + 1])
        finish(*unit, s_cur)


def _attention(h, bias, sink_log2):
    s = h.shape[0]
    tq = ATTN_TQ
    r = tq // BLOCK
    nb = s // BLOCK
    assert nb >= 2
    gw = Q_PER_KV * HEAD_DIM

    def prev_map(col):
        return lambda hk, i: (jnp.maximum(i * r - 1, 0), col + hk)

    def cur_map(col):
        return lambda hk, i: (i, col + hk)

    def next_map(col):
        return lambda hk, i: (jnp.minimum((i + 1) * r, nb - 1), col + hk)

    return pl.pallas_call(
        _attn_kernel,
        grid=(N_KV_HEADS, s // tq),
        in_specs=[
            pl.BlockSpec(memory_space=pltpu.SMEM),
            pl.BlockSpec((tq, gw), lambda hk, i: (i, hk)),
            pl.BlockSpec((BLOCK, HEAD_DIM), prev_map(K_COL)),
            pl.BlockSpec((tq, HEAD_DIM), cur_map(K_COL)),
            pl.BlockSpec((BLOCK, HEAD_DIM), next_map(K_COL)),
            pl.BlockSpec((BLOCK, HEAD_DIM), prev_map(V_COL)),
            pl.BlockSpec((tq, HEAD_DIM), cur_map(V_COL)),
            pl.BlockSpec((BLOCK, HEAD_DIM), next_map(V_COL)),
            pl.BlockSpec((1, 3, Q_PER_KV, BLOCK, 3 * BLOCK), lambda hk, i: (hk, 0, 0, 0, 0)),
        ],
        out_specs=pl.BlockSpec((tq, gw), lambda hk, i: (i, hk)),
        out_shape=jax.ShapeDtypeStruct((s, ATTN_WIDTH), BF16),
        scratch_shapes=[
            pltpu.VMEM((tq + 2 * BLOCK, HEAD_DIM), BF16),
            pltpu.VMEM((tq + 2 * BLOCK, 2 * HEAD_DIM), BF16),
        ],
        compiler_params=pltpu.CompilerParams(
            dimension_semantics=("parallel", "parallel"),
            vmem_limit_bytes=V7X_VMEM_LIMIT_BYTES),
        name="banded_attention",
    )(sink_log2, h, h, h, h, h, h, h, bias)


def _attn_bias_table():
    a = np.arange(BLOCK)[:, None]
    c = np.arange(3 * BLOCK)[None, :]
    dist = np.abs(a + BLOCK - c)
    slopes = np.exp2(-8.0 * np.arange(1, N_Q_HEADS + 1, dtype=np.float32) / N_Q_HEADS)
    bias = (-slopes[:, None, None] * dist.astype(np.float32)[None]
            * np.float32(LOG2E))
    in_window = dist <= WINDOW
    variants = np.stack([in_window & (c >= BLOCK), in_window, in_window & (c < 2 * BLOCK)])
    table = np.where(variants[None], bias[:, None], np.float32(NEG_INF))
    table = table.reshape(N_KV_HEADS, Q_PER_KV, 3, BLOCK, 3 * BLOCK)
    return np.ascontiguousarray(table.transpose(0, 2, 1, 3, 4), dtype=np.float32)


def _pool_band_matrices(tm):
    t = np.arange(tm)[:, None]
    col = np.arange(tm + 2 * POOL_HALO)[None, :]
    pos = np.where(col < tm + POOL_HALO, col, col - tm - 2 * POOL_HALO)
    mats = []
    for w in POOL_WINDOWS:
        lo = t - w // 2
        hi = t + w - w // 2
        mats.append(((pos >= lo) & (pos < hi)).astype(np.float32))
    return np.stack(mats)


def _mixer_out_kernel(o_ref, pp_ref, pc_ref, pn_ref, ga_ref, gb_ref, x_ref, band_ref, wgrp_ref,
                      pscale_ref, wpa_ref, wpp_ref, wout_ref, g_ref, b_ref,
                      out_ref, pcat_ref, zpre_ref):
    i = pl.program_id(0)
    n_tiles = pl.num_programs(0) - 1
    tm = pc_ref.shape[0]
    seq = n_tiles * tm
    slab = tm // MIX_NORM_SLABS

    def norm_slab(k):
        rows = slice(k * slab, (k + 1) * slab)
        out_ref[rows, :] = _layer_norm(zpre_ref[rows, :], g_ref[...], b_ref[...])

    def mix_tile(with_norm):
        zeros_halo = jnp.zeros((POOL_HALO, POOL_WIDTH), BF16)
        pcat_ref[0:tm, :] = pc_ref[...]
        pcat_ref[tm:tm + POOL_HALO, :] = jnp.where(i == n_tiles - 1, zeros_halo, pn_ref[...])
        pcat_ref[tm + POOL_HALO:, :] = jnp.where(i == 0, zeros_halo, pp_ref[...])

        if with_norm:
            norm_slab(0)
        y_a = jnp.dot(o_ref[...], wpa_ref[...], preferred_element_type=F32)
        if with_norm:
            norm_slab(1)

        t = i * tm + lax.broadcasted_iota(jnp.int32, (tm, 1), 0)
        groups = range(N_POOL_GROUPS)
        cols = [slice(g * POOL_GROUP_DIM, (g + 1) * POOL_GROUP_DIM) for g in groups]
        win_sums = [jnp.dot(band_ref[g], pcat_ref[:, cols[g]], preferred_element_type=F32)
                    for g in groups]
        diffs = []
        for g, w in enumerate(POOL_WINDOWS):
            lo = jnp.clip(t - w // 2, 0, seq)
            hi = jnp.clip(t + w - w // 2, 0, seq)
            mean = win_sums[g] / (hi - lo).astype(F32)
            diffs.append((mean - pc_ref[:, cols[g]].astype(F32)).astype(BF16))
        maps = [jnp.dot(diffs[g], wgrp_ref[g], preferred_element_type=F32) for g in groups]
        mixed = jnp.concatenate(
            [(maps[g] * pscale_ref[:, cols[g]]).astype(BF16) for g in groups], axis=-1)

        y_b = jnp.dot(mixed, wpp_ref[...], preferred_element_type=F32)
        merged = (_sigmoid(ga_ref[...].astype(F32)) * y_a
                  + _sigmoid(gb_ref[...].astype(F32)) * y_b)
        if with_norm:
            norm_slab(2)
        z = jnp.dot(merged.astype(BF16), wout_ref[...], preferred_element_type=F32)
        if with_norm:
            norm_slab(3)
        zpre_ref[...] = ALPHA * x_ref[...] + z

    pl.when(i == 0)(lambda: mix_tile(False))
    pl.when((i > 0) & (i < n_tiles))(lambda: mix_tile(True))

    @pl.when(i == n_tiles)
    def _():
        for k in range(MIX_NORM_SLABS):
            norm_slab(k)


def _mixer_out(o, h, x, band, wgrp, pscale, wpa, wpp, wout, g, b):
    s, d = x.shape
    tm = MIX_TM
    hr = tm // POOL_HALO
    n_halo = s // POOL_HALO
    pool_cb = POOL_OFF // POOL_WIDTH
    ga_cb = GATE_A_OFF // D_MODEL
    gb_cb = GATE_B_OFF // D_MODEL
    assert POOL_OFF % POOL_WIDTH == 0 and GATE_A_OFF % D_MODEL == 0 and GATE_B_OFF % D_MODEL == 0

    n = s // tm
    assert tm % (8 * MIX_NORM_SLABS) == 0

    def const(shape):
        return pl.BlockSpec(shape, lambda i: (0,) * len(shape), pipeline_mode=pl.Buffered(1))

    def tile(i):
        return jnp.minimum(i, n - 1)

    return pl.pallas_call(
        _mixer_out_kernel,
        grid=(n + 1,),
        in_specs=[
            pl.BlockSpec((tm, ATTN_WIDTH), lambda i: (tile(i), 0)),
            pl.BlockSpec((POOL_HALO, POOL_WIDTH),
                         lambda i: (jnp.maximum(tile(i) * hr - 1, 0), pool_cb)),
            pl.BlockSpec((tm, POOL_WIDTH), lambda i: (tile(i), pool_cb)),
            pl.BlockSpec((POOL_HALO, POOL_WIDTH),
                         lambda i: (jnp.minimum((tile(i) + 1) * hr, n_halo - 1), pool_cb)),
            pl.BlockSpec((tm, D_MODEL), lambda i: (tile(i), ga_cb)),
            pl.BlockSpec((tm, D_MODEL), lambda i: (tile(i), gb_cb)),
            pl.BlockSpec((tm, d), lambda i: (tile(i), 0)),
            const(band.shape),
            const(wgrp.shape),
            const(pscale.shape),
            const(wpa.shape),
            const(wpp.shape),
            const(wout.shape),
            const(g.shape),
            const(b.shape),
        ],
        out_specs=pl.BlockSpec((tm, d), lambda i: (jnp.maximum(i - 1, 0), 0)),
        out_shape=jax.ShapeDtypeStruct((s, d), F32),
        scratch_shapes=[pltpu.VMEM((tm + 2 * POOL_HALO, POOL_WIDTH), BF16),
                        pltpu.VMEM((tm, d), F32)],
        compiler_params=pltpu.CompilerParams(
            dimension_semantics=("arbitrary",),
            vmem_limit_bytes=V7X_VMEM_LIMIT_BYTES),
        name="mixer_out_ln",
    )(o, h, h, h, h, h, x, band, wgrp, pscale, wpa, wpp, wout, g, b)


def kernel(x, ffn1_w_gate, ffn1_w_up, ffn1_w_down, ln1_g, ln1_b, w_in, attn_sink, pool_w_groups,
           pool_scale, w_proj_attn, w_proj_pool, w_out, ln2_g, ln2_b, ffn2_w_gate, ffn2_w_up,
           ffn2_w_down, ln3_g, ln3_b):
    batch, seq, d = x.shape
    assert (batch, seq, d) == (1, SEQ, D_MODEL)
    x = x.reshape(seq, d)

    bias = jnp.asarray(_attn_bias_table())
    band = jnp.asarray(_pool_band_matrices(MIX_TM), BF16)
    depth = ffn1_w_gate.shape[0]
    for l in range(depth):
        x, xb, w_in_b = _ffn_ln(
            x, ffn1_w_gate[l].astype(BF16), ffn1_w_up[l].astype(BF16),
            ffn1_w_down[l].astype(BF16), ln1_g[l][None], ln1_b[l][None], "ffn1_ln1", True,
            side_casts=(w_in[l],))
        h, wg2, wu2, wd2, wpa, wpp, wout = _in_proj(
            xb, w_in_b, side_casts=(ffn2_w_gate[l], ffn2_w_up[l], ffn2_w_down[l],
                                    w_proj_attn[l], w_proj_pool[l], w_out[l]))
        o = _attention(h, bias, attn_sink[l].astype(F32) * LOG2E)
        x = _mixer_out(o, h, x, band, pool_w_groups[l].astype(BF16), pool_scale[l][None],
                       wpa, wpp, wout, ln2_g[l][None], ln2_b[l][None])
        (x,) = _ffn_ln(x, wg2, wu2, wd2, ln3_g[l][None], ln3_b[l][None], "ffn2_ln3", False)
    return x.reshape(batch, seq, d)
```

```python
import functools
import math

import numpy as np
import jax
import jax.numpy as jnp
from jax import lax
from jax.experimental import pallas as pl
from jax.experimental.pallas import tpu as pltpu

D_MODEL = 2048
SEQ = 16384
HEAD_DIM = 128
N_Q_HEADS = 16
N_KV_HEADS = 4
Q_PER_KV = N_Q_HEADS // N_KV_HEADS
ATTN_WIDTH = N_Q_HEADS * HEAD_DIM
KV_WIDTH = N_KV_HEADS * HEAD_DIM
WINDOW = 128
BLOCK = 128
POOL_WINDOWS = (2, 4, 8, 16)
N_POOL_GROUPS = len(POOL_WINDOWS)
POOL_GROUP_DIM = 256
POOL_WIDTH = N_POOL_GROUPS * POOL_GROUP_DIM
D_FF = 5632
ALPHA = 2.0 ** 0.25
LN_EPS = 1e-5
NEG_INF = -1e30
LOG2E = math.log2(math.e)
IN_WIDTH = ATTN_WIDTH + 2 * KV_WIDTH + POOL_WIDTH + 2 * D_MODEL

Q_COL = 0
K_COL = ATTN_WIDTH // HEAD_DIM
V_COL = (ATTN_WIDTH + KV_WIDTH) // HEAD_DIM
POOL_OFF = ATTN_WIDTH + 2 * KV_WIDTH
GATE_A_OFF = POOL_OFF + POOL_WIDTH
GATE_B_OFF = GATE_A_OFF + D_MODEL

V7X_VMEM_LIMIT_BYTES = 56 * 1024 * 1024
POOL_HALO = 16

FFN_TM = 512
FFN_TF = 512
FFN_HEAD_TF = 256
FFN_NORM_STEPS = 8
FFN_X_PIECES = 4
FFN_X_SWITCH = (2, 3, 6, 7)
PROJ_TM = 2048
PROJ_TN = 1024
ATTN_TQ = 1024
MIX_TM = 256
MIX_NORM_SLABS = 4

F32 = jnp.float32
BF16 = jnp.bfloat16


def _layer_norm(y, g, b):
    mu = jnp.mean(y, axis=-1, keepdims=True)
    yc = y - mu
    var = jnp.mean(yc * yc, axis=-1, keepdims=True)
    return yc * lax.rsqrt(var + LN_EPS) * g + b


def _sigmoid(x):
    return 1.0 / (1.0 + jnp.exp(-x))


def _ffn_head_kernel(x_ref, wg_ref, wu_ref, wd_ref, g_ref, b_ref,
                     y_ref, yb_ref, wgb_ref, wub_ref, wdb_ref, xb_ref, acc_ref):
    j = pl.program_id(0)

    @pl.when(j == 0)
    def _():
        x = x_ref[...]
        xb_ref[...] = x.astype(BF16)
        acc_ref[...] = (2.0 * ALPHA) * x

    wg, wu, wd = (r[...].astype(BF16) for r in (wg_ref, wu_ref, wd_ref))
    wgb_ref[...] = wg
    wub_ref[...] = wu
    wdb_ref[...] = wd
    xb = xb_ref[...]
    gate = jnp.dot(xb, wg, preferred_element_type=F32)
    up = jnp.dot(xb, wu, preferred_element_type=F32)
    act = (gate * _sigmoid(gate) * up).astype(BF16)
    acc_ref[...] += jnp.dot(act, wd, preferred_element_type=F32)

    @pl.when(j == pl.num_programs(0) - 1)
    def _():
        y = _layer_norm(0.5 * acc_ref[...], g_ref[...], b_ref[...])
        y_ref[...] = y
        yb_ref[...] = y.astype(BF16)


def _ffn_head(x, wg, wu, wd, g, b):
    s, d = x.shape
    f = wg.shape[1]
    tm, tf = FFN_TM, FFN_HEAD_TF
    nj = f // tf
    assert nj * tf == f
    return pl.pallas_call(
        _ffn_head_kernel,
        grid=(nj,),
        in_specs=[
            pl.BlockSpec((tm, d), lambda j: (0, 0)),
            pl.BlockSpec((d, tf), lambda j: (0, j)),
            pl.BlockSpec((d, tf), lambda j: (0, j)),
            pl.BlockSpec((tf, d), lambda j: (j, 0)),
            pl.BlockSpec((1, d), lambda j: (0, 0)),
            pl.BlockSpec((1, d), lambda j: (0, 0)),
        ],
        out_specs=[
            pl.BlockSpec((tm, d), lambda j: (0, 0)),
            pl.BlockSpec((tm, d), lambda j: (0, 0)),
            pl.BlockSpec((d, tf), lambda j: (0, j)),
            pl.BlockSpec((d, tf), lambda j: (0, j)),
            pl.BlockSpec((tf, d), lambda j: (j, 0)),
        ],
        out_shape=[
            jax.ShapeDtypeStruct((s, d), F32),
            jax.ShapeDtypeStruct((s, d), BF16),
            jax.ShapeDtypeStruct(wg.shape, BF16),
            jax.ShapeDtypeStruct(wu.shape, BF16),
            jax.ShapeDtypeStruct(wd.shape, BF16),
        ],
        scratch_shapes=[pltpu.VMEM((tm, d), BF16), pltpu.VMEM((tm, d), F32)],
        compiler_params=pltpu.CompilerParams(
            dimension_semantics=("arbitrary",),
            vmem_limit_bytes=V7X_VMEM_LIMIT_BYTES),
        name="ffn1_head",
    )(x, wg, wu, wd, g, b)


def _ffn_ln_kernel(n_out, n_side, *refs):
    x_refs = refs[:FFN_X_PIECES]
    wg_ref, wu_ref, wd_ref, g_ref, b_ref = refs[FFN_X_PIECES:FFN_X_PIECES + 5]
    rest = refs[FFN_X_PIECES + 5:]
    side_in = rest[:n_side]
    rest = rest[n_side:]
    xb_ref, acc_ref, ypre_ref = rest[-3:]
    rest = rest[:-3]
    out_refs = rest[-(n_out + n_side):-n_side] if n_side else rest[-n_out:]
    side_out = rest[len(rest) - n_side:]
    i = pl.program_id(0)
    j = pl.program_id(1)
    n_tiles = pl.num_programs(0) - 1
    last_j = pl.num_programs(1) - 1
    slab = xb_ref.shape[0] // FFN_NORM_STEPS
    slabs_per_half = FFN_NORM_STEPS // 2

    def down_proj(xb):
        gate = jnp.dot(xb, wg_ref[...], preferred_element_type=F32)
        up = jnp.dot(xb, wu_ref[...], preferred_element_type=F32)
        act = (gate * _sigmoid(gate) * up).astype(BF16)
        return jnp.dot(act, wd_ref[...], preferred_element_type=F32)

    def norm_slab():
        rows = pl.ds(pl.multiple_of(j * slab, slab), slab)
        half_rows = pl.ds(pl.multiple_of((j % slabs_per_half) * slab, slab), slab)
        y = _layer_norm(ypre_ref[rows, :], g_ref[...], b_ref[...])
        out_refs[0][half_rows, :] = y
        if n_out == 2:
            out_refs[1][half_rows, :] = y.astype(BF16)

    def first_chunk(with_norm):
        if with_norm:
            norm_slab()
        x = jnp.concatenate([r[...] for r in x_refs], axis=0)
        xb = x.astype(BF16)
        xb_ref[...] = xb
        acc_ref[...] = (2.0 * ALPHA) * x + down_proj(xb)

    def middle_chunk(with_norm):
        if with_norm:
            norm_slab()
        acc_ref[...] += down_proj(xb_ref[...])

    real_tile = i < n_tiles
    norm_due = (i > 0) & (j < FFN_NORM_STEPS)
    middle = (j > 0) & (j < last_j)
    pl.when((j == 0) & (i == 0))(lambda: first_chunk(False))
    pl.when((j == 0) & (i > 0) & real_tile)(lambda: first_chunk(True))
    pl.when(middle & real_tile & norm_due)(lambda: middle_chunk(True))
    pl.when(middle & real_tile & jnp.logical_not(norm_due))(lambda: middle_chunk(False))
    pl.when(jnp.logical_not(real_tile) & norm_due)(norm_slab)

    def cast_side_blocks():
        for src, dst in zip(side_in, side_out):
            dst[...] = src[...].astype(BF16)

    @pl.when((j == last_j) & real_tile)
    def _():
        ypre_ref[...] = 0.5 * (acc_ref[...] + down_proj(xb_ref[...]))
        cast_side_blocks()

    if n_side:
        pl.when((j == last_j) & jnp.logical_not(real_tile))(cast_side_blocks)


def _ffn_ln(x, wg, wu, wd, g, b, name, with_bf16_copy, side_casts=(), head=None):
    s, d = x.shape
    f = wg.shape[1]
    tm, tf = FFN_TM, FFN_TF
    first = 0 if head is None else 1
    n, nj = s // tm - first, f // tf
    n_out = 2 if with_bf16_copy else 1
    head = () if head is None else tuple(head)
    assert len(head) in (0, n_out)
    assert FFN_NORM_STEPS < nj and tm % (16 * FFN_NORM_STEPS) == 0

    def wcol(i, j):
        return jnp.where(i < n, j, nj - 1)

    side_specs, side_shapes = [], []
    for w in side_casts:
        n_blocks = n + first
        rows = w.shape[0] // n_blocks
        assert rows * n_blocks == w.shape[0] and rows % 16 == 0
        side_specs.append(pl.BlockSpec((rows, w.shape[1]),
                                       lambda i, j: (jnp.minimum(i, n_blocks - 1), 0)))
        side_shapes.append(jax.ShapeDtypeStruct(w.shape, BF16))
    piece = tm // FFN_X_PIECES
    assert piece * FFN_X_PIECES == tm and len(FFN_X_SWITCH) == FFN_X_PIECES
    assert 0 < min(FFN_X_SWITCH) and max(FFN_X_SWITCH) < nj

    def x_spec(p):
        def index_map(i, j):
            tile = jnp.minimum(i + (j >= FFN_X_SWITCH[p]).astype(jnp.int32), n - 1) + first
            return (tile * FFN_X_PIECES + p, 0)
        return pl.BlockSpec((piece, d), index_map)

    half_steps = FFN_NORM_STEPS // 2

    def out_index(i, j):
        done = (j >= half_steps).astype(jnp.int32) + (j >= 2 * half_steps).astype(jnp.int32)
        half = jnp.where(i == 0, 0, jnp.minimum(2 * (i - 1) + done, 2 * n - 1))
        return (half + 2 * first, 0)

    out_spec = pl.BlockSpec((tm // 2, d), out_index)
    n_in = FFN_X_PIECES + 5 + len(side_casts)
    return pl.pallas_call(
        functools.partial(_ffn_ln_kernel, n_out, len(side_casts)),
        grid=(n + 1, nj),
        in_specs=[x_spec(p) for p in range(FFN_X_PIECES)] + [
            pl.BlockSpec((d, tf), lambda i, j: (0, wcol(i, j))),
            pl.BlockSpec((d, tf), lambda i, j: (0, wcol(i, j))),
            pl.BlockSpec((tf, d), lambda i, j: (wcol(i, j), 0)),
            pl.BlockSpec((1, d), lambda i, j: (0, 0)),
            pl.BlockSpec((1, d), lambda i, j: (0, 0)),
        ] + side_specs + [pl.BlockSpec(memory_space=pl.ANY)] * len(head),
        input_output_aliases={n_in + k: k for k in range(len(head))},
        out_specs=[out_spec] * n_out + side_specs,
        out_shape=[jax.ShapeDtypeStruct((s, d), F32),
                   jax.ShapeDtypeStruct((s, d), BF16)][:n_out] + side_shapes,
        scratch_shapes=[pltpu.VMEM((tm, d), BF16), pltpu.VMEM((tm, d), F32),
                        pltpu.VMEM((tm, d), F32)],
        compiler_params=pltpu.CompilerParams(
            dimension_semantics=("arbitrary", "arbitrary"),
            vmem_limit_bytes=V7X_VMEM_LIMIT_BYTES),
        name=name,
    )(*([x] * FFN_X_PIECES), wg, wu, wd, g, b, *side_casts, *head)


def _proj_kernel(repeats, x_ref, w_ref, *rest):
    n_side = len(repeats)
    side_in = rest[:n_side]
    o_ref = rest[n_side]
    side_out = rest[n_side + 1:]
    step = pl.program_id(0) * pl.num_programs(1) + pl.program_id(1)

    o_ref[...] = jnp.dot(x_ref[...], w_ref[...], preferred_element_type=F32).astype(o_ref.dtype)

    for src, dst, rep in zip(side_in, side_out, repeats):
        if rep == 1:
            dst[...] = src[...].astype(BF16)
        else:
            @pl.when(step % rep == 0)
            def _(src=src, dst=dst):
                dst[...] = src[...].astype(BF16)


def _in_proj(xb, w, side_casts):
    s, d = xb.shape
    n = w.shape[1]
    tm, tn = PROJ_TM, PROJ_TN
    gi, gj = s // tm, n // tn
    steps = gi * gj

    def step_map(rep):
        return lambda i, j: ((i * gj + j) // rep, 0)

    side_specs, side_shapes, repeats = [], [], []
    for a in side_casts:
        n_blocks = steps
        while a.shape[0] % (16 * n_blocks):
            n_blocks //= 2
        assert n_blocks >= 1 and steps % n_blocks == 0
        repeats.append(steps // n_blocks)
        side_specs.append(pl.BlockSpec((a.shape[0] // n_blocks, a.shape[1]),
                                       step_map(repeats[-1])))
        side_shapes.append(jax.ShapeDtypeStruct(a.shape, BF16))
    return pl.pallas_call(
        functools.partial(_proj_kernel, tuple(repeats)),
        grid=(gi, gj),
        in_specs=[
            pl.BlockSpec((tm, d), lambda i, j: (i, 0)),
            pl.BlockSpec((d, tn), lambda i, j: (0, j)),
        ] + side_specs,
        out_specs=[pl.BlockSpec((tm, tn), lambda i, j: (i, j))] + side_specs,
        out_shape=[jax.ShapeDtypeStruct((s, n), BF16)] + side_shapes,
        compiler_params=pltpu.CompilerParams(
            dimension_semantics=("arbitrary", "arbitrary"),
            vmem_limit_bytes=V7X_VMEM_LIMIT_BYTES),
        name="mixer_in_proj",
    )(xb, w, *side_casts)


def _attn_kernel(sink_ref, q_ref, kp_ref, kc_ref, kn_ref, vp_ref, vc_ref, vn_ref, bias_ref,
                 o_ref, kcat_ref, vcat_ref):
    hk = pl.program_id(0)
    i = pl.program_id(1)
    tq = q_ref.shape[0]
    blocks_per_tile = tq // BLOCK
    n_blocks = pl.num_programs(1) * blocks_per_tile

    kcat_ref[0:BLOCK, :] = kp_ref[...]
    kcat_ref[BLOCK:BLOCK + tq, :] = kc_ref[...]
    kcat_ref[BLOCK + tq:, :] = kn_ref[...]
    vcat_ref[0:BLOCK, 0:HEAD_DIM] = vp_ref[...]
    vcat_ref[BLOCK:BLOCK + tq, 0:HEAD_DIM] = vc_ref[...]
    vcat_ref[BLOCK + tq:, 0:HEAD_DIM] = vn_ref[...]
    vcat_ref[:, HEAD_DIM:] = jnp.ones((tq + 2 * BLOCK, HEAD_DIM), BF16)

    qk_scale = LOG2E / math.sqrt(HEAD_DIM)

    def bias_variant(b):
        if b == 0:
            return jnp.where(i == 0, 0, 1)
        if b == blocks_per_tile - 1:
            return jnp.where(i == pl.num_programs(1) - 1, 2, 1)
        return 1

    def scores(b, g):
        q = q_ref[b * BLOCK:(b + 1) * BLOCK, g * HEAD_DIM:(g + 1) * HEAD_DIM]
        kw = kcat_ref[b * BLOCK:(b + 3) * BLOCK, :]
        return lax.dot_general(q, kw, (((1,), (1,)), ((), ())), preferred_element_type=F32)

    def finish(b, g, s):
        sink = sink_ref[hk * Q_PER_KV + g]
        s = s * qk_scale + bias_ref[0, bias_variant(b), g]
        m = jnp.maximum(jnp.max(s, axis=-1, keepdims=True), sink)
        m_b = jnp.broadcast_to(m, (BLOCK, HEAD_DIM))
        p = jnp.exp2(s - jnp.concatenate([m_b] * 3, axis=-1))
        vw = vcat_ref[b * BLOCK:(b + 3) * BLOCK, :]
        ov = jnp.dot(p.astype(BF16), vw, preferred_element_type=F32)
        denom = ov[:, HEAD_DIM:] + jnp.exp2(sink - m_b)
        o_ref[b * BLOCK:(b + 1) * BLOCK, g * HEAD_DIM:(g + 1) * HEAD_DIM] = (
            ov[:, :HEAD_DIM] / denom).astype(o_ref.dtype)

    units = [(b, g) for b in range(blocks_per_tile) for g in range(Q_PER_KV)]
    s_next = scores(*units[0])
    for u, unit in enumerate(units):
        s_cur = s_next
        if u + 1 < len(units):
            s_next = scores(*units[u + 1])
        finish(*unit, s_cur)


def _attention(h, bias, sink_log2):
    s = h.shape[0]
    tq = ATTN_TQ
    r = tq // BLOCK
    nb = s // BLOCK
    assert nb >= 2
    gw = Q_PER_KV * HEAD_DIM

    def prev_map(col):
        return lambda hk, i: (jnp.maximum(i * r - 1, 0), col + hk)

    def cur_map(col):
        return lambda hk, i: (i, col + hk)

    def next_map(col):
        return lambda hk, i: (jnp.minimum((i + 1) * r, nb - 1), col + hk)

    return pl.pallas_call(
        _attn_kernel,
        grid=(N_KV_HEADS, s // tq),
        in_specs=[
            pl.BlockSpec(memory_space=pltpu.SMEM),
            pl.BlockSpec((tq, gw), lambda hk, i: (i, hk)),
            pl.BlockSpec((BLOCK, HEAD_DIM), prev_map(K_COL)),
            pl.BlockSpec((tq, HEAD_DIM), cur_map(K_COL)),
            pl.BlockSpec((BLOCK, HEAD_DIM), next_map(K_COL)),
            pl.BlockSpec((BLOCK, HEAD_DIM), prev_map(V_COL)),
            pl.BlockSpec((tq, HEAD_DIM), cur_map(V_COL)),
            pl.BlockSpec((BLOCK, HEAD_DIM), next_map(V_COL)),
            pl.BlockSpec((1, 3, Q_PER_KV, BLOCK, 3 * BLOCK), lambda hk, i: (hk, 0, 0, 0, 0)),
        ],
        out_specs=pl.BlockSpec((tq, gw), lambda hk, i: (i, hk)),
        out_shape=jax.ShapeDtypeStruct((s, ATTN_WIDTH), BF16),
        scratch_shapes=[
            pltpu.VMEM((tq + 2 * BLOCK, HEAD_DIM), BF16),
            pltpu.VMEM((tq + 2 * BLOCK, 2 * HEAD_DIM), BF16),
        ],
        compiler_params=pltpu.CompilerParams(
            dimension_semantics=("parallel", "parallel"),
            vmem_limit_bytes=V7X_VMEM_LIMIT_BYTES),
        name="banded_attention",
    )(sink_log2, h, h, h, h, h, h, h, bias)


def _attn_bias_table():
    a = np.arange(BLOCK)[:, None]
    c = np.arange(3 * BLOCK)[None, :]
    dist = np.abs(a + BLOCK - c)
    slopes = np.exp2(-8.0 * np.arange(1, N_Q_HEADS + 1, dtype=np.float32) / N_Q_HEADS)
    bias = (-slopes[:, None, None] * dist.astype(np.float32)[None]
            * np.float32(LOG2E))
    in_window = dist <= WINDOW
    variants = np.stack([in_window & (c >= BLOCK), in_window, in_window & (c < 2 * BLOCK)])
    table = np.where(variants[None], bias[:, None], np.float32(NEG_INF))
    table = table.reshape(N_KV_HEADS, Q_PER_KV, 3, BLOCK, 3 * BLOCK)
    return np.ascontiguousarray(table.transpose(0, 2, 1, 3, 4), dtype=np.float32)


def _pool_band_matrices(tm):
    t = np.arange(tm)[:, None]
    col = np.arange(tm + 2 * POOL_HALO)[None, :]
    pos = np.where(col < tm + POOL_HALO, col, col - tm - 2 * POOL_HALO)
    mats = []
    for w in POOL_WINDOWS:
        lo = t - w // 2
        hi = t + w - w // 2
        mats.append(((pos >= lo) & (pos < hi)).astype(np.float32))
    return np.stack(mats)


def _mixer_out_kernel(o_ref, pp_ref, pc_ref, pn_ref, ga_ref, gb_ref, x_ref, band_ref, wgrp_ref,
                      pscale_ref, wpa_ref, wpp_ref, wout_ref, g_ref, b_ref,
                      out_ref, pcat_ref, zpre_ref):
    i = pl.program_id(0)
    n_tiles = pl.num_programs(0) - 1
    tm = pc_ref.shape[0]
    seq = n_tiles * tm
    slab = tm // MIX_NORM_SLABS

    def norm_slab(k):
        rows = slice(k * slab, (k + 1) * slab)
        out_ref[rows, :] = _layer_norm(zpre_ref[rows, :], g_ref[...], b_ref[...])

    def mix_tile(with_norm):
        zeros_halo = jnp.zeros((POOL_HALO, POOL_WIDTH), BF16)
        pcat_ref[0:tm, :] = pc_ref[...]
        pcat_ref[tm:tm + POOL_HALO, :] = jnp.where(i == n_tiles - 1, zeros_halo, pn_ref[...])
        pcat_ref[tm + POOL_HALO:, :] = jnp.where(i == 0, zeros_halo, pp_ref[...])

        if with_norm:
            norm_slab(0)
        y_a = jnp.dot(o_ref[...], wpa_ref[...], preferred_element_type=F32)
        if with_norm:
            norm_slab(1)

        t = i * tm + lax.broadcasted_iota(jnp.int32, (tm, 1), 0)
        groups = range(N_POOL_GROUPS)
        cols = [slice(g * POOL_GROUP_DIM, (g + 1) * POOL_GROUP_DIM) for g in groups]
        win_sums = [jnp.dot(band_ref[g], pcat_ref[:, cols[g]], preferred_element_type=F32)
                    for g in groups]
        diffs = []
        for g, w in enumerate(POOL_WINDOWS):
            lo = jnp.clip(t - w // 2, 0, seq)
            hi = jnp.clip(t + w - w // 2, 0, seq)
            mean = win_sums[g] / (hi - lo).astype(F32)
            diffs.append((mean - pc_ref[:, cols[g]].astype(F32)).astype(BF16))
        maps = [jnp.dot(diffs[g], wgrp_ref[g], preferred_element_type=F32) for g in groups]
        mixed = jnp.concatenate(
            [(maps[g] * pscale_ref[:, cols[g]]).astype(BF16) for g in groups], axis=-1)

        y_b = jnp.dot(mixed, wpp_ref[...], preferred_element_type=F32)
        merged = (_sigmoid(ga_ref[...].astype(F32)) * y_a
                  + _sigmoid(gb_ref[...].astype(F32)) * y_b)
        if with_norm:
            norm_slab(2)
        z = jnp.dot(merged.astype(BF16), wout_ref[...], preferred_element_type=F32)
        if with_norm:
            norm_slab(3)
        zpre_ref[...] = ALPHA * x_ref[...] + z

    pl.when(i == 0)(lambda: mix_tile(False))
    pl.when((i > 0) & (i < n_tiles))(lambda: mix_tile(True))

    @pl.when(i == n_tiles)
    def _():
        for k in range(MIX_NORM_SLABS):
            norm_slab(k)


def _mixer_out(o, h, x, band, wgrp, pscale, wpa, wpp, wout, g, b):
    s, d = x.shape
    tm = MIX_TM
    hr = tm // POOL_HALO
    n_halo = s // POOL_HALO
    pool_cb = POOL_OFF // POOL_WIDTH
    ga_cb = GATE_A_OFF // D_MODEL
    gb_cb = GATE_B_OFF // D_MODEL
    assert POOL_OFF % POOL_WIDTH == 0 and GATE_A_OFF % D_MODEL == 0 and GATE_B_OFF % D_MODEL == 0

    n = s // tm
    assert tm % (8 * MIX_NORM_SLABS) == 0

    def const(shape):
        return pl.BlockSpec(shape, lambda i: (0,) * len(shape), pipeline_mode=pl.Buffered(1))

    def tile(i):
        return jnp.minimum(i, n - 1)

    return pl.pallas_call(
        _mixer_out_kernel,
        grid=(n + 1,),
        in_specs=[
            pl.BlockSpec((tm, ATTN_WIDTH), lambda i: (tile(i), 0)),
            pl.BlockSpec((POOL_HALO, POOL_WIDTH),
                         lambda i: (jnp.maximum(tile(i) * hr - 1, 0), pool_cb)),
            pl.BlockSpec((tm, POOL_WIDTH), lambda i: (tile(i), pool_cb)),
            pl.BlockSpec((POOL_HALO, POOL_WIDTH),
                         lambda i: (jnp.minimum((tile(i) + 1) * hr, n_halo - 1), pool_cb)),
            pl.BlockSpec((tm, D_MODEL), lambda i: (tile(i), ga_cb)),
            pl.BlockSpec((tm, D_MODEL), lambda i: (tile(i), gb_cb)),
            pl.BlockSpec((tm, d), lambda i: (tile(i), 0)),
            const(band.shape),
            const(wgrp.shape),
            const(pscale.shape),
            const(wpa.shape),
            const(wpp.shape),
            const(wout.shape),
            const(g.shape),
            const(b.shape),
        ],
        out_specs=pl.BlockSpec((tm, d), lambda i: (jnp.maximum(i - 1, 0), 0)),
        out_shape=jax.ShapeDtypeStruct((s, d), F32),
        scratch_shapes=[pltpu.VMEM((tm + 2 * POOL_HALO, POOL_WIDTH), BF16),
                        pltpu.VMEM((tm, d), F32)],
        compiler_params=pltpu.CompilerParams(
            dimension_semantics=("arbitrary",),
            vmem_limit_bytes=V7X_VMEM_LIMIT_BYTES),
        name="mixer_out_ln",
    )(o, h, h, h, h, h, x, band, wgrp, pscale, wpa, wpp, wout, g, b)


def kernel(x, ffn1_w_gate, ffn1_w_up, ffn1_w_down, ln1_g, ln1_b, w_in, attn_sink, pool_w_groups,
           pool_scale, w_proj_attn, w_proj_pool, w_out, ln2_g, ln2_b, ffn2_w_gate, ffn2_w_up,
           ffn2_w_down, ln3_g, ln3_b):
    batch, seq, d = x.shape
    assert (batch, seq, d) == (1, SEQ, D_MODEL)
    x = x.reshape(seq, d)

    bias = jnp.asarray(_attn_bias_table())
    band = jnp.asarray(_pool_band_matrices(MIX_TM), BF16)
    depth = ffn1_w_gate.shape[0]
    for l in range(depth):
        y_head, yb_head, wg1, wu1, wd1 = _ffn_head(
            x, ffn1_w_gate[l], ffn1_w_up[l], ffn1_w_down[l], ln1_g[l][None], ln1_b[l][None])
        x, xb, w_in_b = _ffn_ln(
            x, wg1, wu1, wd1, ln1_g[l][None], ln1_b[l][None], "ffn1_ln1", True,
            side_casts=(w_in[l],), head=(y_head, yb_head))
        h, wg2, wu2, wd2, wpa, wpp, wout = _in_proj(
            xb, w_in_b, side_casts=(ffn2_w_gate[l], ffn2_w_up[l], ffn2_w_down[l],
                                    w_proj_attn[l], w_proj_pool[l], w_out[l]))
        o = _attention(h, bias, attn_sink[l].astype(F32) * LOG2E)
        x = _mixer_out(o, h, x, band, pool_w_groups[l].astype(BF16), pool_scale[l][None],
                       wpa, wpp, wout, ln2_g[l][None], ln2_b[l][None])
        (x,) = _ffn_ln(x, wg2, wu2, wd2, ln3_g[l][None], ln3_b[l][None], "ffn2_ln3", False)
    return x.reshape(batch, seq, d)
```

```python
import functools
import math

import numpy as np
import jax
import jax.numpy as jnp
from jax import lax
from jax.experimental import pallas as pl
from jax.experimental.pallas import tpu as pltpu

D_MODEL = 2048
SEQ = 16384
HEAD_DIM = 128
N_Q_HEADS = 16
N_KV_HEADS = 4
Q_PER_KV = N_Q_HEADS // N_KV_HEADS
ATTN_WIDTH = N_Q_HEADS * HEAD_DIM
KV_WIDTH = N_KV_HEADS * HEAD_DIM
WINDOW = 128
BLOCK = 128
POOL_WINDOWS = (2, 4, 8, 16)
N_POOL_GROUPS = len(POOL_WINDOWS)
POOL_GROUP_DIM = 256
POOL_WIDTH = N_POOL_GROUPS * POOL_GROUP_DIM
D_FF = 5632
ALPHA = 2.0 ** 0.25
LN_EPS = 1e-5
NEG_INF = -1e30
LOG2E = math.log2(math.e)
IN_WIDTH = ATTN_WIDTH + 2 * KV_WIDTH + POOL_WIDTH + 2 * D_MODEL

Q_COL = 0
K_COL = ATTN_WIDTH // HEAD_DIM
V_COL = (ATTN_WIDTH + KV_WIDTH) // HEAD_DIM
POOL_OFF = ATTN_WIDTH + 2 * KV_WIDTH
GATE_A_OFF = POOL_OFF + POOL_WIDTH
GATE_B_OFF = GATE_A_OFF + D_MODEL

V7X_VMEM_LIMIT_BYTES = 56 * 1024 * 1024
POOL_HALO = 16

FFN_TM = 512
FFN_TF = 512
FFN_HEAD_TF = 256
FFN_NORM_STEPS = 8
FFN_X_PIECES = 2
FFN_X_SWITCH = (3, 7)
PROJ_TM = 2048
PROJ_TN = 1024
ATTN_TQ = 1024
MIX_TM = 256
MIX_NORM_SLABS = 4

F32 = jnp.float32
BF16 = jnp.bfloat16


def _layer_norm(y, g, b):
    mu = jnp.mean(y, axis=-1, keepdims=True)
    yc = y - mu
    var = jnp.mean(yc * yc, axis=-1, keepdims=True)
    return yc * lax.rsqrt(var + LN_EPS) * g + b


def _sigmoid(x):
    return 1.0 / (1.0 + jnp.exp(-x))


def _ffn_head_kernel(x_ref, wg_ref, wu_ref, wd_ref, g_ref, b_ref,
                     y_ref, yb_ref, wgb_ref, wub_ref, wdb_ref, xb_ref, acc_ref):
    j = pl.program_id(0)

    @pl.when(j == 0)
    def _():
        x = x_ref[...]
        xb_ref[...] = x.astype(BF16)
        acc_ref[...] = (2.0 * ALPHA) * x

    wg, wu, wd = (r[...].astype(BF16) for r in (wg_ref, wu_ref, wd_ref))
    wgb_ref[...] = wg
    wub_ref[...] = wu
    wdb_ref[...] = wd
    xb = xb_ref[...]
    gate = jnp.dot(xb, wg, preferred_element_type=F32)
    up = jnp.dot(xb, wu, preferred_element_type=F32)
    act = (gate * _sigmoid(gate) * up).astype(BF16)
    acc_ref[...] += jnp.dot(act, wd, preferred_element_type=F32)

    @pl.when(j == pl.num_programs(0) - 1)
    def _():
        y = _layer_norm(0.5 * acc_ref[...], g_ref[...], b_ref[...])
        y_ref[...] = y
        yb_ref[...] = y.astype(BF16)


def _ffn_head(x, wg, wu, wd, g, b):
    s, d = x.shape
    f = wg.shape[1]
    tm, tf = FFN_TM, FFN_HEAD_TF
    nj = f // tf
    assert nj * tf == f
    return pl.pallas_call(
        _ffn_head_kernel,
        grid=(nj,),
        in_specs=[
            pl.BlockSpec((tm, d), lambda j: (0, 0)),
            pl.BlockSpec((d, tf), lambda j: (0, j)),
            pl.BlockSpec((d, tf), lambda j: (0, j)),
            pl.BlockSpec((tf, d), lambda j: (j, 0)),
            pl.BlockSpec((1, d), lambda j: (0, 0)),
            pl.BlockSpec((1, d), lambda j: (0, 0)),
        ],
        out_specs=[
            pl.BlockSpec((tm, d), lambda j: (0, 0)),
            pl.BlockSpec((tm, d), lambda j: (0, 0)),
            pl.BlockSpec((d, tf), lambda j: (0, j)),
            pl.BlockSpec((d, tf), lambda j: (0, j)),
            pl.BlockSpec((tf, d), lambda j: (j, 0)),
        ],
        out_shape=[
            jax.ShapeDtypeStruct((s, d), F32),
            jax.ShapeDtypeStruct((s, d), BF16),
            jax.ShapeDtypeStruct(wg.shape, BF16),
            jax.ShapeDtypeStruct(wu.shape, BF16),
            jax.ShapeDtypeStruct(wd.shape, BF16),
        ],
        scratch_shapes=[pltpu.VMEM((tm, d), BF16), pltpu.VMEM((tm, d), F32)],
        compiler_params=pltpu.CompilerParams(
            dimension_semantics=("arbitrary",),
            vmem_limit_bytes=V7X_VMEM_LIMIT_BYTES),
        name="ffn1_head",
    )(x, wg, wu, wd, g, b)


def _ffn_ln_kernel(n_out, n_side, *refs):
    x_refs = refs[:FFN_X_PIECES]
    wg_ref, wu_ref, wd_ref, g_ref, b_ref = refs[FFN_X_PIECES:FFN_X_PIECES + 5]
    rest = refs[FFN_X_PIECES + 5:]
    side_in = rest[:n_side]
    rest = rest[n_side:]
    xb_ref, acc_ref, ypre_ref = rest[-3:]
    rest = rest[:-3]
    out_refs = rest[-(n_out + n_side):-n_side] if n_side else rest[-n_out:]
    side_out = rest[len(rest) - n_side:]
    i = pl.program_id(0)
    j = pl.program_id(1)
    n_tiles = pl.num_programs(0) - 1
    last_j = pl.num_programs(1) - 1
    slab = xb_ref.shape[0] // FFN_NORM_STEPS
    slabs_per_half = FFN_NORM_STEPS // 2

    def down_proj(xb):
        gate = jnp.dot(xb, wg_ref[...], preferred_element_type=F32)
        up = jnp.dot(xb, wu_ref[...], preferred_element_type=F32)
        act = (gate * _sigmoid(gate) * up).astype(BF16)
        return jnp.dot(act, wd_ref[...], preferred_element_type=F32)

    def norm_slab():
        rows = pl.ds(pl.multiple_of(j * slab, slab), slab)
        half_rows = pl.ds(pl.multiple_of((j % slabs_per_half) * slab, slab), slab)
        y = _layer_norm(ypre_ref[rows, :], g_ref[...], b_ref[...])
        out_refs[0][half_rows, :] = y
        if n_out == 2:
            out_refs[1][half_rows, :] = y.astype(BF16)

    def first_chunk(with_norm):
        if with_norm:
            norm_slab()
        x = jnp.concatenate([r[...] for r in x_refs], axis=0)
        xb = x.astype(BF16)
        xb_ref[...] = xb
        acc_ref[...] = (2.0 * ALPHA) * x + down_proj(xb)

    def middle_chunk(with_norm):
        if with_norm:
            norm_slab()
        acc_ref[...] += down_proj(xb_ref[...])

    real_tile = i < n_tiles
    norm_due = (i > 0) & (j < FFN_NORM_STEPS)
    middle = (j > 0) & (j < last_j)
    pl.when((j == 0) & (i == 0))(lambda: first_chunk(False))
    pl.when((j == 0) & (i > 0) & real_tile)(lambda: first_chunk(True))
    pl.when(middle & real_tile & norm_due)(lambda: middle_chunk(True))
    pl.when(middle & real_tile & jnp.logical_not(norm_due))(lambda: middle_chunk(False))
    pl.when(jnp.logical_not(real_tile) & norm_due)(norm_slab)

    def cast_side_blocks():
        for src, dst in zip(side_in, side_out):
            dst[...] = src[...].astype(BF16)

    @pl.when((j == last_j) & real_tile)
    def _():
        ypre_ref[...] = 0.5 * (acc_ref[...] + down_proj(xb_ref[...]))
        cast_side_blocks()

    if n_side:
        pl.when((j == last_j) & jnp.logical_not(real_tile))(cast_side_blocks)


def _ffn_ln(x, wg, wu, wd, g, b, name, with_bf16_copy, side_casts=(), head=None):
    s, d = x.shape
    f = wg.shape[1]
    tm, tf = FFN_TM, FFN_TF
    first = 0 if head is None else 1
    n, nj = s // tm - first, f // tf
    n_out = 2 if with_bf16_copy else 1
    head = () if head is None else tuple(head)
    assert len(head) in (0, n_out)
    assert FFN_NORM_STEPS < nj and tm % (16 * FFN_NORM_STEPS) == 0

    def wcol(i, j):
        return jnp.where(i < n, j, nj - 1)

    side_specs, side_shapes = [], []
    for w in side_casts:
        n_blocks = n + first
        rows = w.shape[0] // n_blocks
        assert rows * n_blocks == w.shape[0] and rows % 16 == 0
        side_specs.append(pl.BlockSpec((rows, w.shape[1]),
                                       lambda i, j: (jnp.minimum(i, n_blocks - 1), 0)))
        side_shapes.append(jax.ShapeDtypeStruct(w.shape, BF16))
    piece = tm // FFN_X_PIECES
    assert piece * FFN_X_PIECES == tm and len(FFN_X_SWITCH) == FFN_X_PIECES
    assert 0 < min(FFN_X_SWITCH) and max(FFN_X_SWITCH) < nj

    def x_spec(p):
        def index_map(i, j):
            tile = jnp.minimum(i + (j >= FFN_X_SWITCH[p]).astype(jnp.int32), n - 1) + first
            return (tile * FFN_X_PIECES + p, 0)
        return pl.BlockSpec((piece, d), index_map)

    half_steps = FFN_NORM_STEPS // 2

    def out_index(i, j):
        done = (j >= half_steps).astype(jnp.int32) + (j >= 2 * half_steps).astype(jnp.int32)
        half = jnp.where(i == 0, 0, jnp.minimum(2 * (i - 1) + done, 2 * n - 1))
        return (half + 2 * first, 0)

    out_spec = pl.BlockSpec((tm // 2, d), out_index)
    n_in = FFN_X_PIECES + 5 + len(side_casts)
    return pl.pallas_call(
        functools.partial(_ffn_ln_kernel, n_out, len(side_casts)),
        grid=(n + 1, nj),
        in_specs=[x_spec(p) for p in range(FFN_X_PIECES)] + [
            pl.BlockSpec((d, tf), lambda i, j: (0, wcol(i, j))),
            pl.BlockSpec((d, tf), lambda i, j: (0, wcol(i, j))),
            pl.BlockSpec((tf, d), lambda i, j: (wcol(i, j), 0)),
            pl.BlockSpec((1, d), lambda i, j: (0, 0)),
            pl.BlockSpec((1, d), lambda i, j: (0, 0)),
        ] + side_specs + [pl.BlockSpec(memory_space=pl.ANY)] * len(head),
        input_output_aliases={n_in + k: k for k in range(len(head))},
        out_specs=[out_spec] * n_out + side_specs,
        out_shape=[jax.ShapeDtypeStruct((s, d), F32),
                   jax.ShapeDtypeStruct((s, d), BF16)][:n_out] + side_shapes,
        scratch_shapes=[pltpu.VMEM((tm, d), BF16), pltpu.VMEM((tm, d), F32),
                        pltpu.VMEM((tm, d), F32)],
        compiler_params=pltpu.CompilerParams(
            dimension_semantics=("arbitrary", "arbitrary"),
            vmem_limit_bytes=V7X_VMEM_LIMIT_BYTES),
        name=name,
    )(*([x] * FFN_X_PIECES), wg, wu, wd, g, b, *side_casts, *head)


def _proj_kernel(repeats, x_ref, w_ref, *rest):
    n_side = len(repeats)
    side_in = rest[:n_side]
    o_ref = rest[n_side]
    side_out = rest[n_side + 1:]
    step = pl.program_id(0) * pl.num_programs(1) + pl.program_id(1)

    o_ref[...] = jnp.dot(x_ref[...], w_ref[...], preferred_element_type=F32).astype(o_ref.dtype)

    for src, dst, rep in zip(side_in, side_out, repeats):
        if rep == 1:
            dst[...] = src[...].astype(BF16)
        else:
            @pl.when(step % rep == 0)
            def _(src=src, dst=dst):
                dst[...] = src[...].astype(BF16)


def _in_proj(xb, w, side_casts):
    s, d = xb.shape
    n = w.shape[1]
    tm, tn = PROJ_TM, PROJ_TN
    gi, gj = s // tm, n // tn
    steps = gi * gj

    def step_map(rep):
        return lambda i, j: ((i * gj + j) // rep, 0)

    side_specs, side_shapes, repeats = [], [], []
    for a in side_casts:
        n_blocks = steps
        while a.shape[0] % (16 * n_blocks):
            n_blocks //= 2
        assert n_blocks >= 1 and steps % n_blocks == 0
        repeats.append(steps // n_blocks)
        side_specs.append(pl.BlockSpec((a.shape[0] // n_blocks, a.shape[1]),
                                       step_map(repeats[-1])))
        side_shapes.append(jax.ShapeDtypeStruct(a.shape, BF16))
    return pl.pallas_call(
        functools.partial(_proj_kernel, tuple(repeats)),
        grid=(gi, gj),
        in_specs=[
            pl.BlockSpec((tm, d), lambda i, j: (i, 0)),
            pl.BlockSpec((d, tn), lambda i, j: (0, j)),
        ] + side_specs,
        out_specs=[pl.BlockSpec((tm, tn), lambda i, j: (i, j))] + side_specs,
        out_shape=[jax.ShapeDtypeStruct((s, n), BF16)] + side_shapes,
        compiler_params=pltpu.CompilerParams(
            dimension_semantics=("arbitrary", "arbitrary"),
            vmem_limit_bytes=V7X_VMEM_LIMIT_BYTES),
        name="mixer_in_proj",
    )(xb, w, *side_casts)


def _attn_kernel(sink_ref, q_ref, kp_ref, kc_ref, kn_ref, vp_ref, vc_ref, vn_ref, bias_ref,
                 o_ref, kcat_ref, vcat_ref):
    hk = pl.program_id(0)
    i = pl.program_id(1)
    tq = q_ref.shape[0]
    blocks_per_tile = tq // BLOCK
    n_blocks = pl.num_programs(1) * blocks_per_tile

    kcat_ref[0:BLOCK, :] = kp_ref[...]
    kcat_ref[BLOCK:BLOCK + tq, :] = kc_ref[...]
    kcat_ref[BLOCK + tq:, :] = kn_ref[...]
    vcat_ref[0:BLOCK, 0:HEAD_DIM] = vp_ref[...]
    vcat_ref[BLOCK:BLOCK + tq, 0:HEAD_DIM] = vc_ref[...]
    vcat_ref[BLOCK + tq:, 0:HEAD_DIM] = vn_ref[...]
    vcat_ref[:, HEAD_DIM:] = jnp.ones((tq + 2 * BLOCK, HEAD_DIM), BF16)

    qk_scale = LOG2E / math.sqrt(HEAD_DIM)

    def bias_variant(b):
        if b == 0:
            return jnp.where(i == 0, 0, 1)
        if b == blocks_per_tile - 1:
            return jnp.where(i == pl.num_programs(1) - 1, 2, 1)
        return 1

    def scores(b, g):
        q = q_ref[b * BLOCK:(b + 1) * BLOCK, g * HEAD_DIM:(g + 1) * HEAD_DIM]
        kw = kcat_ref[b * BLOCK:(b + 3) * BLOCK, :]
        return lax.dot_general(q, kw, (((1,), (1,)), ((), ())), preferred_element_type=F32)

    def finish(b, g, s):
        sink = sink_ref[hk * Q_PER_KV + g]
        s = s * qk_scale + bias_ref[0, bias_variant(b), g]
        m = jnp.maximum(jnp.max(s, axis=-1, keepdims=True), sink)
        m_b = jnp.broadcast_to(m, (BLOCK, HEAD_DIM))
        p = jnp.exp2(s - jnp.concatenate([m_b] * 3, axis=-1))
        vw = vcat_ref[b * BLOCK:(b + 3) * BLOCK, :]
        ov = jnp.dot(p.astype(BF16), vw, preferred_element_type=F32)
        denom = ov[:, HEAD_DIM:] + jnp.exp2(sink - m_b)
        o_ref[b * BLOCK:(b + 1) * BLOCK, g * HEAD_DIM:(g + 1) * HEAD_DIM] = (
            ov[:, :HEAD_DIM] / denom).astype(o_ref.dtype)

    units = [(b, g) for b in range(blocks_per_tile) for g in range(Q_PER_KV)]
    s_next = scores(*units[0])
    for u, unit in enumerate(units):
        s_cur = s_next
        if u + 1 < len(units):
            s_next = scores(*units[u + 1])
        finish(*unit, s_cur)


def _attention(h, bias, sink_log2):
    s = h.shape[0]
    tq = ATTN_TQ
    r = tq // BLOCK
    nb = s // BLOCK
    assert nb >= 2
    gw = Q_PER_KV * HEAD_DIM

    def prev_map(col):
        return lambda hk, i: (jnp.maximum(i * r - 1, 0), col + hk)

    def cur_map(col):
        return lambda hk, i: (i, col + hk)

    def next_map(col):
        return lambda hk, i: (jnp.minimum((i + 1) * r, nb - 1), col + hk)

    return pl.pallas_call(
        _attn_kernel,
        grid=(N_KV_HEADS, s // tq),
        in_specs=[
            pl.BlockSpec(memory_space=pltpu.SMEM),
            pl.BlockSpec((tq, gw), lambda hk, i: (i, hk)),
            pl.BlockSpec((BLOCK, HEAD_DIM), prev_map(K_COL)),
            pl.BlockSpec((tq, HEAD_DIM), cur_map(K_COL)),
            pl.BlockSpec((BLOCK, HEAD_DIM), next_map(K_COL)),
            pl.BlockSpec((BLOCK, HEAD_DIM), prev_map(V_COL)),
            pl.BlockSpec((tq, HEAD_DIM), cur_map(V_COL)),
            pl.BlockSpec((BLOCK, HEAD_DIM), next_map(V_COL)),
            pl.BlockSpec((1, 3, Q_PER_KV, BLOCK, 3 * BLOCK), lambda hk, i: (hk, 0, 0, 0, 0)),
        ],
        out_specs=pl.BlockSpec((tq, gw), lambda hk, i: (i, hk)),
        out_shape=jax.ShapeDtypeStruct((s, ATTN_WIDTH), BF16),
        scratch_shapes=[
            pltpu.VMEM((tq + 2 * BLOCK, HEAD_DIM), BF16),
            pltpu.VMEM((tq + 2 * BLOCK, 2 * HEAD_DIM), BF16),
        ],
        compiler_params=pltpu.CompilerParams(
            dimension_semantics=("parallel", "parallel"),
            vmem_limit_bytes=V7X_VMEM_LIMIT_BYTES),
        name="banded_attention",
    )(sink_log2, h, h, h, h, h, h, h, bias)


def _attn_bias_table():
    a = np.arange(BLOCK)[:, None]
    c = np.arange(3 * BLOCK)[None, :]
    dist = np.abs(a + BLOCK - c)
    slopes = np.exp2(-8.0 * np.arange(1, N_Q_HEADS + 1, dtype=np.float32) / N_Q_HEADS)
    bias = (-slopes[:, None, None] * dist.astype(np.float32)[None]
            * np.float32(LOG2E))
    in_window = dist <= WINDOW
    variants = np.stack([in_window & (c >= BLOCK), in_window, in_window & (c < 2 * BLOCK)])
    table = np.where(variants[None], bias[:, None], np.float32(NEG_INF))
    table = table.reshape(N_KV_HEADS, Q_PER_KV, 3, BLOCK, 3 * BLOCK)
    return np.ascontiguousarray(table.transpose(0, 2, 1, 3, 4), dtype=np.float32)


def _pool_band_matrices(tm):
    t = np.arange(tm)[:, None]
    col = np.arange(tm + 2 * POOL_HALO)[None, :]
    pos = np.where(col < tm + POOL_HALO, col, col - tm - 2 * POOL_HALO)
    mats = []
    for w in POOL_WINDOWS:
        lo = t - w // 2
        hi = t + w - w // 2
        mats.append(((pos >= lo) & (pos < hi)).astype(np.float32))
    return np.stack(mats)


def _mixer_out_kernel(o_ref, pp_ref, pc_ref, pn_ref, ga_ref, gb_ref, x_ref, band_ref, wgrp_ref,
                      pscale_ref, wpa_ref, wpp_ref, wout_ref, g_ref, b_ref,
                      out_ref, pcat_ref, zpre_ref):
    i = pl.program_id(0)
    n_tiles = pl.num_programs(0) - 1
    tm = pc_ref.shape[0]
    seq = n_tiles * tm
    slab = tm // MIX_NORM_SLABS

    def norm_slab(k):
        rows = slice(k * slab, (k + 1) * slab)
        out_ref[rows, :] = _layer_norm(zpre_ref[rows, :], g_ref[...], b_ref[...])

    def mix_tile(with_norm):
        zeros_halo = jnp.zeros((POOL_HALO, POOL_WIDTH), BF16)
        pcat_ref[0:tm, :] = pc_ref[...]
        pcat_ref[tm:tm + POOL_HALO, :] = jnp.where(i == n_tiles - 1, zeros_halo, pn_ref[...])
        pcat_ref[tm + POOL_HALO:, :] = jnp.where(i == 0, zeros_halo, pp_ref[...])

        if with_norm:
            norm_slab(0)
        y_a = jnp.dot(o_ref[...], wpa_ref[...], preferred_element_type=F32)
        if with_norm:
            norm_slab(1)

        t = i * tm + lax.broadcasted_iota(jnp.int32, (tm, 1), 0)
        groups = range(N_POOL_GROUPS)
        cols = [slice(g * POOL_GROUP_DIM, (g + 1) * POOL_GROUP_DIM) for g in groups]
        win_sums = [jnp.dot(band_ref[g], pcat_ref[:, cols[g]], preferred_element_type=F32)
                    for g in groups]
        diffs = []
        for g, w in enumerate(POOL_WINDOWS):
            lo = jnp.clip(t - w // 2, 0, seq)
            hi = jnp.clip(t + w - w // 2, 0, seq)
            mean = win_sums[g] / (hi - lo).astype(F32)
            diffs.append((mean - pc_ref[:, cols[g]].astype(F32)).astype(BF16))
        maps = [jnp.dot(diffs[g], wgrp_ref[g], preferred_element_type=F32) for g in groups]
        mixed = jnp.concatenate(
            [(maps[g] * pscale_ref[:, cols[g]]).astype(BF16) for g in groups], axis=-1)

        y_b = jnp.dot(mixed, wpp_ref[...], preferred_element_type=F32)
        merged = (_sigmoid(ga_ref[...].astype(F32)) * y_a
                  + _sigmoid(gb_ref[...].astype(F32)) * y_b)
        if with_norm:
            norm_slab(2)
        z = jnp.dot(merged.astype(BF16), wout_ref[...], preferred_element_type=F32)
        if with_norm:
            norm_slab(3)
        zpre_ref[...] = ALPHA * x_ref[...] + z

    pl.when(i == 0)(lambda: mix_tile(False))
    pl.when((i > 0) & (i < n_tiles))(lambda: mix_tile(True))

    @pl.when(i == n_tiles)
    def _():
        for k in range(MIX_NORM_SLABS):
            norm_slab(k)


def _mixer_out(o, h, x, band, wgrp, pscale, wpa, wpp, wout, g, b):
    s, d = x.shape
    tm = MIX_TM
    hr = tm // POOL_HALO
    n_halo = s // POOL_HALO
    pool_cb = POOL_OFF // POOL_WIDTH
    ga_cb = GATE_A_OFF // D_MODEL
    gb_cb = GATE_B_OFF // D_MODEL
    assert POOL_OFF % POOL_WIDTH == 0 and GATE_A_OFF % D_MODEL == 0 and GATE_B_OFF % D_MODEL == 0

    n = s // tm
    assert tm % (8 * MIX_NORM_SLABS) == 0

    def const(shape):
        return pl.BlockSpec(shape, lambda i: (0,) * len(shape), pipeline_mode=pl.Buffered(1))

    def tile(i):
        return jnp.minimum(i, n - 1)

    return pl.pallas_call(
        _mixer_out_kernel,
        grid=(n + 1,),
        in_specs=[
            pl.BlockSpec((tm, ATTN_WIDTH), lambda i: (tile(i), 0)),
            pl.BlockSpec((POOL_HALO, POOL_WIDTH),
                         lambda i: (jnp.maximum(tile(i) * hr - 1, 0), pool_cb)),
            pl.BlockSpec((tm, POOL_WIDTH), lambda i: (tile(i), pool_cb)),
            pl.BlockSpec((POOL_HALO, POOL_WIDTH),
                         lambda i: (jnp.minimum((tile(i) + 1) * hr, n_halo - 1), pool_cb)),
            pl.BlockSpec((tm, D_MODEL), lambda i: (tile(i), ga_cb)),
            pl.BlockSpec((tm, D_MODEL), lambda i: (tile(i), gb_cb)),
            pl.BlockSpec((tm, d), lambda i: (tile(i), 0)),
            const(band.shape),
            const(wgrp.shape),
            const(pscale.shape),
            const(wpa.shape),
            const(wpp.shape),
            const(wout.shape),
            const(g.shape),
            const(b.shape),
        ],
        out_specs=pl.BlockSpec((tm, d), lambda i: (jnp.maximum(i - 1, 0), 0)),
        out_shape=jax.ShapeDtypeStruct((s, d), F32),
        scratch_shapes=[pltpu.VMEM((tm + 2 * POOL_HALO, POOL_WIDTH), BF16),
                        pltpu.VMEM((tm, d), F32)],
        compiler_params=pltpu.CompilerParams(
            dimension_semantics=("arbitrary",),
            vmem_limit_bytes=V7X_VMEM_LIMIT_BYTES),
        name="mixer_out_ln",
    )(o, h, h, h, h, h, x, band, wgrp, pscale, wpa, wpp, wout, g, b)


def kernel(x, ffn1_w_gate, ffn1_w_up, ffn1_w_down, ln1_g, ln1_b, w_in, attn_sink, pool_w_groups,
           pool_scale, w_proj_attn, w_proj_pool, w_out, ln2_g, ln2_b, ffn2_w_gate, ffn2_w_up,
           ffn2_w_down, ln3_g, ln3_b):
    batch, seq, d = x.shape
    assert (batch, seq, d) == (1, SEQ, D_MODEL)
    x = x.reshape(seq, d)

    bias = jnp.asarray(_attn_bias_table())
    band = jnp.asarray(_pool_band_matrices(MIX_TM), BF16)
    depth = ffn1_w_gate.shape[0]
    for l in range(depth):
        y_head, yb_head, wg1, wu1, wd1 = _ffn_head(
            x, ffn1_w_gate[l], ffn1_w_up[l], ffn1_w_down[l], ln1_g[l][None], ln1_b[l][None])
        x, xb, w_in_b = _ffn_ln(
            x, wg1, wu1, wd1, ln1_g[l][None], ln1_b[l][None], "ffn1_ln1", True,
            side_casts=(w_in[l],), head=(y_head, yb_head))
        h, wg2, wu2, wd2, wpa, wpp, wout = _in_proj(
            xb, w_in_b, side_casts=(ffn2_w_gate[l], ffn2_w_up[l], ffn2_w_down[l],
                                    w_proj_attn[l], w_proj_pool[l], w_out[l]))
        o = _attention(h, bias, attn_sink[l].astype(F32) * LOG2E)
        x = _mixer_out(o, h, x, band, pool_w_groups[l].astype(BF16), pool_scale[l][None],
                       wpa, wpp, wout, ln2_g[l][None], ln2_b[l][None])
        (x,) = _ffn_ln(x, wg2, wu2, wd2, ln3_g[l][None], ln3_b[l][None], "ffn2_ln3", False)
    return x.reshape(batch, seq, d)
```

```python
import functools
import math

import numpy as np
import jax
import jax.numpy as jnp
from jax import lax
from jax.experimental import pallas as pl
from jax.experimental.pallas import tpu as pltpu

D_MODEL = 2048
SEQ = 16384
HEAD_DIM = 128
N_Q_HEADS = 16
N_KV_HEADS = 4
Q_PER_KV = N_Q_HEADS // N_KV_HEADS
ATTN_WIDTH = N_Q_HEADS * HEAD_DIM
KV_WIDTH = N_KV_HEADS * HEAD_DIM
WINDOW = 128
BLOCK = 128
POOL_WINDOWS = (2, 4, 8, 16)
N_POOL_GROUPS = len(POOL_WINDOWS)
POOL_GROUP_DIM = 256
POOL_WIDTH = N_POOL_GROUPS * POOL_GROUP_DIM
ALPHA = 2.0 ** 0.25
LN_EPS = 1e-5
NEG_INF = -1e30
LOG2E = math.log2(math.e)

K_COL = ATTN_WIDTH // HEAD_DIM
V_COL = (ATTN_WIDTH + KV_WIDTH) // HEAD_DIM
POOL_OFF = ATTN_WIDTH + 2 * KV_WIDTH
GATE_A_OFF = POOL_OFF + POOL_WIDTH
GATE_B_OFF = GATE_A_OFF + D_MODEL

V7X_VMEM_LIMIT_BYTES = 56 * 1024 * 1024
F32_TILE_ROWS = 8
BF16_TILE_ROWS = 16
POOL_HALO = BF16_TILE_ROWS

FFN_TM = 512
FFN_TF = 512
FFN_HEAD_TF = 256
FFN_NORM_SLAB = 64
FFN_X_PIECES = 2
FFN_X_SWITCH = (3, 7)
PROJ_TM = 2048
PROJ_TN = 1024
ATTN_TQ = 1024
MIX_TM = 256
MIX_NORM_SLABS = 4

F32 = jnp.float32
BF16 = jnp.bfloat16


def _layer_norm(y, g, b):
    mu = jnp.mean(y, axis=-1, keepdims=True)
    yc = y - mu
    var = jnp.mean(yc * yc, axis=-1, keepdims=True)
    return yc * lax.rsqrt(var + LN_EPS) * g + b


def _sigmoid(x):
    return 1.0 / (1.0 + jnp.exp(-x))


def _ffn_head_kernel(x_ref, wg_ref, wu_ref, wd_ref, g_ref, b_ref,
                     y_ref, yb_ref, wgb_ref, wub_ref, wdb_ref, xb_ref, acc_ref):
    j = pl.program_id(0)

    @pl.when(j == 0)
    def _():
        x = x_ref[...]
        xb_ref[...] = x.astype(BF16)
        acc_ref[...] = (2.0 * ALPHA) * x

    wg, wu, wd = (r[...].astype(BF16) for r in (wg_ref, wu_ref, wd_ref))
    wgb_ref[...] = wg
    wub_ref[...] = wu
    wdb_ref[...] = wd
    xb = xb_ref[...]
    gate = jnp.dot(xb, wg, preferred_element_type=F32)
    up = jnp.dot(xb, wu, preferred_element_type=F32)
    act = (gate * _sigmoid(gate) * up).astype(BF16)
    acc_ref[...] += jnp.dot(act, wd, preferred_element_type=F32)

    @pl.when(j == pl.num_programs(0) - 1)
    def _():
        y = _layer_norm(0.5 * acc_ref[...], g_ref[...], b_ref[...])
        y_ref[...] = y
        yb_ref[...] = y.astype(BF16)


def _ffn_head(x, wg, wu, wd, g, b):
    s, d = x.shape
    f = wg.shape[1]
    tm, tf = FFN_TM, FFN_HEAD_TF
    nj = f // tf
    assert nj * tf == f
    return pl.pallas_call(
        _ffn_head_kernel,
        grid=(nj,),
        in_specs=[
            pl.BlockSpec((tm, d), lambda j: (0, 0)),
            pl.BlockSpec((d, tf), lambda j: (0, j)),
            pl.BlockSpec((d, tf), lambda j: (0, j)),
            pl.BlockSpec((tf, d), lambda j: (j, 0)),
            pl.BlockSpec((1, d), lambda j: (0, 0)),
            pl.BlockSpec((1, d), lambda j: (0, 0)),
        ],
        out_specs=[
            pl.BlockSpec((tm, d), lambda j: (0, 0)),
            pl.BlockSpec((tm, d), lambda j: (0, 0)),
            pl.BlockSpec((d, tf), lambda j: (0, j)),
            pl.BlockSpec((d, tf), lambda j: (0, j)),
            pl.BlockSpec((tf, d), lambda j: (j, 0)),
        ],
        out_shape=[
            jax.ShapeDtypeStruct((s, d), F32),
            jax.ShapeDtypeStruct((s, d), BF16),
            jax.ShapeDtypeStruct(wg.shape, BF16),
            jax.ShapeDtypeStruct(wu.shape, BF16),
            jax.ShapeDtypeStruct(wd.shape, BF16),
        ],
        scratch_shapes=[pltpu.VMEM((tm, d), BF16), pltpu.VMEM((tm, d), F32)],
        compiler_params=pltpu.CompilerParams(
            dimension_semantics=("arbitrary",),
            vmem_limit_bytes=V7X_VMEM_LIMIT_BYTES),
        name="ffn1_head",
    )(x, wg, wu, wd, g, b)


def _ffn_ln_kernel(n_out, n_side, *refs):
    x_refs = refs[:FFN_X_PIECES]
    wg_ref, wu_ref, wd_ref, g_ref, b_ref = refs[FFN_X_PIECES:FFN_X_PIECES + 5]
    rest = refs[FFN_X_PIECES + 5:]
    side_in = rest[:n_side]
    rest = rest[n_side:]
    xb_ref, acc_ref, ypre_ref = rest[-3:]
    rest = rest[:-3]
    out_refs = rest[-(n_out + n_side):-n_side] if n_side else rest[-n_out:]
    side_out = rest[len(rest) - n_side:]
    i = pl.program_id(0)
    j = pl.program_id(1)
    n_tiles = pl.num_programs(0) - 1
    last_j = pl.num_programs(1) - 1
    slab = FFN_NORM_SLAB
    norm_steps = xb_ref.shape[0] // slab
    slabs_per_half = norm_steps // 2

    def down_proj(xb):
        gate = jnp.dot(xb, wg_ref[...], preferred_element_type=F32)
        up = jnp.dot(xb, wu_ref[...], preferred_element_type=F32)
        act = (gate * _sigmoid(gate) * up).astype(BF16)
        return jnp.dot(act, wd_ref[...], preferred_element_type=F32)

    def norm_slab():
        rows = pl.ds(pl.multiple_of(j * slab, slab), slab)
        half_rows = pl.ds(pl.multiple_of((j % slabs_per_half) * slab, slab), slab)
        y = _layer_norm(ypre_ref[rows, :], g_ref[...], b_ref[...])
        out_refs[0][half_rows, :] = y
        if n_out == 2:
            out_refs[1][half_rows, :] = y.astype(BF16)

    def first_chunk(with_norm):
        if with_norm:
            norm_slab()
        x = jnp.concatenate([r[...] for r in x_refs], axis=0)
        xb = x.astype(BF16)
        xb_ref[...] = xb
        acc_ref[...] = (2.0 * ALPHA) * x + down_proj(xb)

    def middle_chunk(with_norm):
        if with_norm:
            norm_slab()
        acc_ref[...] += down_proj(xb_ref[...])

    real_tile = i < n_tiles
    norm_due = (i > 0) & (j < norm_steps)
    middle = (j > 0) & (j < last_j)
    pl.when((j == 0) & (i == 0))(lambda: first_chunk(False))
    pl.when((j == 0) & (i > 0) & real_tile)(lambda: first_chunk(True))
    pl.when(middle & real_tile & norm_due)(lambda: middle_chunk(True))
    pl.when(middle & real_tile & jnp.logical_not(norm_due))(lambda: middle_chunk(False))
    pl.when(jnp.logical_not(real_tile) & norm_due)(norm_slab)

    def cast_side_blocks():
        for src, dst in zip(side_in, side_out):
            dst[...] = src[...].astype(BF16)

    @pl.when((j == last_j) & real_tile)
    def _():
        ypre_ref[...] = 0.5 * (acc_ref[...] + down_proj(xb_ref[...]))
        cast_side_blocks()

    if n_side:
        pl.when((j == last_j) & jnp.logical_not(real_tile))(cast_side_blocks)


def _ffn_ln(x, wg, wu, wd, g, b, name, with_bf16_copy, side_casts=(), head=None):
    s, d = x.shape
    f = wg.shape[1]
    tm, tf = FFN_TM, FFN_TF
    first = 0 if head is None else 1
    n, nj = s // tm - first, f // tf
    n_out = 2 if with_bf16_copy else 1
    head = () if head is None else tuple(head)
    assert len(head) in (0, n_out)
    norm_steps = tm // FFN_NORM_SLAB
    assert norm_steps < nj and norm_steps % 2 == 0 and FFN_NORM_SLAB % BF16_TILE_ROWS == 0

    def wcol(i, j):
        return jnp.where(i < n, j, nj - 1)

    side_specs, side_shapes = [], []
    for w in side_casts:
        n_blocks = n + first
        rows = w.shape[0] // n_blocks
        assert rows * n_blocks == w.shape[0] and rows % BF16_TILE_ROWS == 0
        side_specs.append(pl.BlockSpec((rows, w.shape[1]),
                                       lambda i, j: (jnp.minimum(i, n_blocks - 1), 0)))
        side_shapes.append(jax.ShapeDtypeStruct(w.shape, BF16))
    piece = tm // FFN_X_PIECES
    assert piece * FFN_X_PIECES == tm and len(FFN_X_SWITCH) == FFN_X_PIECES
    assert 0 < min(FFN_X_SWITCH) and max(FFN_X_SWITCH) < nj

    def x_spec(p):
        def index_map(i, j):
            tile = jnp.minimum(i + (j >= FFN_X_SWITCH[p]).astype(jnp.int32), n - 1) + first
            return (tile * FFN_X_PIECES + p, 0)
        return pl.BlockSpec((piece, d), index_map)

    half_steps = norm_steps // 2

    def out_index(i, j):
        done = (j >= half_steps).astype(jnp.int32) + (j >= 2 * half_steps).astype(jnp.int32)
        half = jnp.where(i == 0, 0, jnp.minimum(2 * (i - 1) + done, 2 * n - 1))
        return (half + 2 * first, 0)

    out_spec = pl.BlockSpec((tm // 2, d), out_index)
    n_in = FFN_X_PIECES + 5 + len(side_casts)
    return pl.pallas_call(
        functools.partial(_ffn_ln_kernel, n_out, len(side_casts)),
        grid=(n + 1, nj),
        in_specs=[x_spec(p) for p in range(FFN_X_PIECES)] + [
            pl.BlockSpec((d, tf), lambda i, j: (0, wcol(i, j))),
            pl.BlockSpec((d, tf), lambda i, j: (0, wcol(i, j))),
            pl.BlockSpec((tf, d), lambda i, j: (wcol(i, j), 0)),
            pl.BlockSpec((1, d), lambda i, j: (0, 0)),
            pl.BlockSpec((1, d), lambda i, j: (0, 0)),
        ] + side_specs + [pl.BlockSpec(memory_space=pl.ANY)] * len(head),
        input_output_aliases={n_in + k: k for k in range(len(head))},
        out_specs=[out_spec] * n_out + side_specs,
        out_shape=[jax.ShapeDtypeStruct((s, d), F32),
                   jax.ShapeDtypeStruct((s, d), BF16)][:n_out] + side_shapes,
        scratch_shapes=[pltpu.VMEM((tm, d), BF16), pltpu.VMEM((tm, d), F32),
                        pltpu.VMEM((tm, d), F32)],
        compiler_params=pltpu.CompilerParams(
            dimension_semantics=("arbitrary", "arbitrary"),
            vmem_limit_bytes=V7X_VMEM_LIMIT_BYTES),
        name=name,
    )(*([x] * FFN_X_PIECES), wg, wu, wd, g, b, *side_casts, *head)


def _proj_kernel(repeats, x_ref, w_ref, *rest):
    n_side = len(repeats)
    side_in = rest[:n_side]
    o_ref = rest[n_side]
    side_out = rest[n_side + 1:]
    step = pl.program_id(0) * pl.num_programs(1) + pl.program_id(1)

    o_ref[...] = jnp.dot(x_ref[...], w_ref[...], preferred_element_type=F32).astype(o_ref.dtype)

    for src, dst, rep in zip(side_in, side_out, repeats):
        if rep == 1:
            dst[...] = src[...].astype(BF16)
        else:
            @pl.when(step % rep == 0)
            def _(src=src, dst=dst):
                dst[...] = src[...].astype(BF16)


def _in_proj(xb, w, side_casts):
    s, d = xb.shape
    n = w.shape[1]
    tm, tn = PROJ_TM, PROJ_TN
    gi, gj = s // tm, n // tn
    steps = gi * gj

    def step_map(rep):
        return lambda i, j: ((i * gj + j) // rep, 0)

    side_specs, side_shapes, repeats = [], [], []
    for a in side_casts:
        n_blocks = steps
        while a.shape[0] % (BF16_TILE_ROWS * n_blocks):
            n_blocks //= 2
        assert n_blocks >= 1 and steps % n_blocks == 0
        repeats.append(steps // n_blocks)
        side_specs.append(pl.BlockSpec((a.shape[0] // n_blocks, a.shape[1]),
                                       step_map(repeats[-1])))
        side_shapes.append(jax.ShapeDtypeStruct(a.shape, BF16))
    return pl.pallas_call(
        functools.partial(_proj_kernel, tuple(repeats)),
        grid=(gi, gj),
        in_specs=[
            pl.BlockSpec((tm, d), lambda i, j: (i, 0)),
            pl.BlockSpec((d, tn), lambda i, j: (0, j)),
        ] + side_specs,
        out_specs=[pl.BlockSpec((tm, tn), lambda i, j: (i, j))] + side_specs,
        out_shape=[jax.ShapeDtypeStruct((s, n), BF16)] + side_shapes,
        compiler_params=pltpu.CompilerParams(
            dimension_semantics=("arbitrary", "arbitrary"),
            vmem_limit_bytes=V7X_VMEM_LIMIT_BYTES),
        name="mixer_in_proj",
    )(xb, w, *side_casts)


def _attn_kernel(sink_ref, q_ref, kp_ref, kc_ref, kn_ref, vp_ref, vc_ref, vn_ref, bias_ref,
                 o_ref, kcat_ref, vcat_ref):
    hk = pl.program_id(0)
    i = pl.program_id(1)
    tq = q_ref.shape[0]
    blocks_per_tile = tq // BLOCK

    kcat_ref[0:BLOCK, :] = kp_ref[...]
    kcat_ref[BLOCK:BLOCK + tq, :] = kc_ref[...]
    kcat_ref[BLOCK + tq:, :] = kn_ref[...]
    vcat_ref[0:BLOCK, 0:HEAD_DIM] = vp_ref[...]
    vcat_ref[BLOCK:BLOCK + tq, 0:HEAD_DIM] = vc_ref[...]
    vcat_ref[BLOCK + tq:, 0:HEAD_DIM] = vn_ref[...]
    vcat_ref[:, HEAD_DIM:] = jnp.ones((tq + 2 * BLOCK, HEAD_DIM), BF16)

    qk_scale = LOG2E / math.sqrt(HEAD_DIM)

    def bias_variant(b):
        if b == 0:
            return jnp.where(i == 0, 0, 1)
        if b == blocks_per_tile - 1:
            return jnp.where(i == pl.num_programs(1) - 1, 2, 1)
        return 1

    def scores(b, g):
        q = q_ref[b * BLOCK:(b + 1) * BLOCK, g * HEAD_DIM:(g + 1) * HEAD_DIM]
        kw = kcat_ref[b * BLOCK:(b + 3) * BLOCK, :]
        return lax.dot_general(q, kw, (((1,), (1,)), ((), ())), preferred_element_type=F32)

    def finish(b, g, s):
        sink = sink_ref[hk * Q_PER_KV + g]
        s = s * qk_scale + bias_ref[0, bias_variant(b), g]
        m = jnp.maximum(jnp.max(s, axis=-1, keepdims=True), sink)
        m_b = jnp.broadcast_to(m, (BLOCK, HEAD_DIM))
        p = jnp.exp2(s - jnp.concatenate([m_b] * 3, axis=-1))
        vw = vcat_ref[b * BLOCK:(b + 3) * BLOCK, :]
        ov = jnp.dot(p.astype(BF16), vw, preferred_element_type=F32)
        denom = ov[:, HEAD_DIM:] + jnp.exp2(sink - m_b)
        o_ref[b * BLOCK:(b + 1) * BLOCK, g * HEAD_DIM:(g + 1) * HEAD_DIM] = (
            ov[:, :HEAD_DIM] / denom).astype(o_ref.dtype)

    units = [(b, g) for b in range(blocks_per_tile) for g in range(Q_PER_KV)]
    s_next = scores(*units[0])
    for u, unit in enumerate(units):
        s_cur = s_next
        if u + 1 < len(units):
            s_next = scores(*units[u + 1])
        finish(*unit, s_cur)


def _attention(h, bias, sink_log2):
    s = h.shape[0]
    tq = ATTN_TQ
    r = tq // BLOCK
    nb = s // BLOCK
    assert nb >= 2
    gw = Q_PER_KV * HEAD_DIM

    def prev_map(col):
        return lambda hk, i: (jnp.maximum(i * r - 1, 0), col + hk)

    def cur_map(col):
        return lambda hk, i: (i, col + hk)

    def next_map(col):
        return lambda hk, i: (jnp.minimum((i + 1) * r, nb - 1), col + hk)

    return pl.pallas_call(
        _attn_kernel,
        grid=(N_KV_HEADS, s // tq),
        in_specs=[
            pl.BlockSpec(memory_space=pltpu.SMEM),
            pl.BlockSpec((tq, gw), lambda hk, i: (i, hk)),
            pl.BlockSpec((BLOCK, HEAD_DIM), prev_map(K_COL)),
            pl.BlockSpec((tq, HEAD_DIM), cur_map(K_COL)),
            pl.BlockSpec((BLOCK, HEAD_DIM), next_map(K_COL)),
            pl.BlockSpec((BLOCK, HEAD_DIM), prev_map(V_COL)),
            pl.BlockSpec((tq, HEAD_DIM), cur_map(V_COL)),
            pl.BlockSpec((BLOCK, HEAD_DIM), next_map(V_COL)),
            pl.BlockSpec((1, 3, Q_PER_KV, BLOCK, 3 * BLOCK), lambda hk, i: (hk, 0, 0, 0, 0)),
        ],
        out_specs=pl.BlockSpec((tq, gw), lambda hk, i: (i, hk)),
        out_shape=jax.ShapeDtypeStruct((s, ATTN_WIDTH), BF16),
        scratch_shapes=[
            pltpu.VMEM((tq + 2 * BLOCK, HEAD_DIM), BF16),
            pltpu.VMEM((tq + 2 * BLOCK, 2 * HEAD_DIM), BF16),
        ],
        compiler_params=pltpu.CompilerParams(
            dimension_semantics=("parallel", "parallel"),
            vmem_limit_bytes=V7X_VMEM_LIMIT_BYTES),
        name="banded_attention",
    )(sink_log2, h, h, h, h, h, h, h, bias)


def _attn_bias_table():
    a = np.arange(BLOCK)[:, None]
    c = np.arange(3 * BLOCK)[None, :]
    dist = np.abs(a + BLOCK - c)
    slopes = np.exp2(-8.0 * np.arange(1, N_Q_HEADS + 1, dtype=np.float32) / N_Q_HEADS)
    bias = (-slopes[:, None, None] * dist.astype(np.float32)[None]
            * np.float32(LOG2E))
    in_window = dist <= WINDOW
    variants = np.stack([in_window & (c >= BLOCK), in_window, in_window & (c < 2 * BLOCK)])
    table = np.where(variants[None], bias[:, None], np.float32(NEG_INF))
    table = table.reshape(N_KV_HEADS, Q_PER_KV, 3, BLOCK, 3 * BLOCK)
    return np.ascontiguousarray(table.transpose(0, 2, 1, 3, 4), dtype=np.float32)


def _pool_band_matrices(tm):
    t = np.arange(tm)[:, None]
    col = np.arange(tm + 2 * POOL_HALO)[None, :]
    pos = np.where(col < tm + POOL_HALO, col, col - tm - 2 * POOL_HALO)
    mats = []
    for w in POOL_WINDOWS:
        lo = t - w // 2
        hi = t + w - w // 2
        mats.append(((pos >= lo) & (pos < hi)).astype(np.float32))
    return np.stack(mats)


def _mixer_out_kernel(o_ref, pp_ref, pc_ref, pn_ref, ga_ref, gb_ref, x_ref, band_ref, wgrp_ref,
                      pscale_ref, wpa_ref, wpp_ref, wout_ref, g_ref, b_ref,
                      out_ref, pcat_ref, zpre_ref):
    i = pl.program_id(0)
    n_tiles = pl.num_programs(0) - 1
    tm = pc_ref.shape[0]
    seq = n_tiles * tm
    slab = tm // MIX_NORM_SLABS

    def norm_slab(k):
        rows = slice(k * slab, (k + 1) * slab)
        out_ref[rows, :] = _layer_norm(zpre_ref[rows, :], g_ref[...], b_ref[...])

    def mix_tile(with_norm):
        zeros_halo = jnp.zeros((POOL_HALO, POOL_WIDTH), BF16)
        pcat_ref[0:tm, :] = pc_ref[...]
        pcat_ref[tm:tm + POOL_HALO, :] = jnp.where(i == n_tiles - 1, zeros_halo, pn_ref[...])
        pcat_ref[tm + POOL_HALO:, :] = jnp.where(i == 0, zeros_halo, pp_ref[...])

        if with_norm:
            norm_slab(0)
        y_a = jnp.dot(o_ref[...], wpa_ref[...], preferred_element_type=F32)
        if with_norm:
            norm_slab(1)

        t = i * tm + lax.broadcasted_iota(jnp.int32, (tm, 1), 0)
        groups = range(N_POOL_GROUPS)
        cols = [slice(g * POOL_GROUP_DIM, (g + 1) * POOL_GROUP_DIM) for g in groups]
        win_sums = [jnp.dot(band_ref[g], pcat_ref[:, cols[g]], preferred_element_type=F32)
                    for g in groups]
        diffs = []
        for g, w in enumerate(POOL_WINDOWS):
            lo = jnp.clip(t - w // 2, 0, seq)
            hi = jnp.clip(t + w - w // 2, 0, seq)
            mean = win_sums[g] / (hi - lo).astype(F32)
            diffs.append((mean - pc_ref[:, cols[g]].astype(F32)).astype(BF16))
        maps = [jnp.dot(diffs[g], wgrp_ref[g], preferred_element_type=F32) for g in groups]
        mixed = jnp.concatenate(
            [(maps[g] * pscale_ref[:, cols[g]]).astype(BF16) for g in groups], axis=-1)

        y_b = jnp.dot(mixed, wpp_ref[...], preferred_element_type=F32)
        merged = (_sigmoid(ga_ref[...].astype(F32)) * y_a
                  + _sigmoid(gb_ref[...].astype(F32)) * y_b)
        if with_norm:
            norm_slab(2)
        z = jnp.dot(merged.astype(BF16), wout_ref[...], preferred_element_type=F32)
        if with_norm:
            norm_slab(3)
        zpre_ref[...] = ALPHA * x_ref[...] + z

    pl.when(i == 0)(lambda: mix_tile(False))
    pl.when((i > 0) & (i < n_tiles))(lambda: mix_tile(True))

    @pl.when(i == n_tiles)
    def _():
        for k in range(MIX_NORM_SLABS):
            norm_slab(k)


def _mixer_out(o, h, x, band, wgrp, pscale, wpa, wpp, wout, g, b):
    s, d = x.shape
    tm = MIX_TM
    hr = tm // POOL_HALO
    n_halo = s // POOL_HALO
    pool_cb = POOL_OFF // POOL_WIDTH
    ga_cb = GATE_A_OFF // D_MODEL
    gb_cb = GATE_B_OFF // D_MODEL
    assert POOL_OFF % POOL_WIDTH == 0 and GATE_A_OFF % D_MODEL == 0 and GATE_B_OFF % D_MODEL == 0

    n = s // tm
    assert tm % (F32_TILE_ROWS * MIX_NORM_SLABS) == 0

    def const(shape):
        return pl.BlockSpec(shape, lambda i: (0,) * len(shape), pipeline_mode=pl.Buffered(1))

    def tile(i):
        return jnp.minimum(i, n - 1)

    return pl.pallas_call(
        _mixer_out_kernel,
        grid=(n + 1,),
        in_specs=[
            pl.BlockSpec((tm, ATTN_WIDTH), lambda i: (tile(i), 0)),
            pl.BlockSpec((POOL_HALO, POOL_WIDTH),
                         lambda i: (jnp.maximum(tile(i) * hr - 1, 0), pool_cb)),
            pl.BlockSpec((tm, POOL_WIDTH), lambda i: (tile(i), pool_cb)),
            pl.BlockSpec((POOL_HALO, POOL_WIDTH),
                         lambda i: (jnp.minimum((tile(i) + 1) * hr, n_halo - 1), pool_cb)),
            pl.BlockSpec((tm, D_MODEL), lambda i: (tile(i), ga_cb)),
            pl.BlockSpec((tm, D_MODEL), lambda i: (tile(i), gb_cb)),
            pl.BlockSpec((tm, d), lambda i: (tile(i), 0)),
            const(band.shape),
            const(wgrp.shape),
            const(pscale.shape),
            const(wpa.shape),
            const(wpp.shape),
            const(wout.shape),
            const(g.shape),
            const(b.shape),
        ],
        out_specs=pl.BlockSpec((tm, d), lambda i: (jnp.maximum(i - 1, 0), 0)),
        out_shape=jax.ShapeDtypeStruct((s, d), F32),
        scratch_shapes=[pltpu.VMEM((tm + 2 * POOL_HALO, POOL_WIDTH), BF16),
                        pltpu.VMEM((tm, d), F32)],
        compiler_params=pltpu.CompilerParams(
            dimension_semantics=("arbitrary",),
            vmem_limit_bytes=V7X_VMEM_LIMIT_BYTES),
        name="mixer_out_ln",
    )(o, h, h, h, h, h, x, band, wgrp, pscale, wpa, wpp, wout, g, b)


def kernel(x, ffn1_w_gate, ffn1_w_up, ffn1_w_down, ln1_g, ln1_b, w_in, attn_sink, pool_w_groups,
           pool_scale, w_proj_attn, w_proj_pool, w_out, ln2_g, ln2_b, ffn2_w_gate, ffn2_w_up,
           ffn2_w_down, ln3_g, ln3_b):
    batch, seq, d = x.shape
    assert (batch, seq, d) == (1, SEQ, D_MODEL)
    x = x.reshape(seq, d)

    bias = jnp.asarray(_attn_bias_table())
    band = jnp.asarray(_pool_band_matrices(MIX_TM), BF16)
    depth = ffn1_w_gate.shape[0]
    for l in range(depth):
        y_head, yb_head, wg1, wu1, wd1 = _ffn_head(
            x, ffn1_w_gate[l], ffn1_w_up[l], ffn1_w_down[l], ln1_g[l][None], ln1_b[l][None])
        x, xb, w_in_b = _ffn_ln(
            x, wg1, wu1, wd1, ln1_g[l][None], ln1_b[l][None], "ffn1_ln1", True,
            side_casts=(w_in[l],), head=(y_head, yb_head))
        h, wg2, wu2, wd2, wpa, wpp, wout = _in_proj(
            xb, w_in_b, side_casts=(ffn2_w_gate[l], ffn2_w_up[l], ffn2_w_down[l],
                                    w_proj_attn[l], w_proj_pool[l], w_out[l]))
        o = _attention(h, bias, attn_sink[l].astype(F32) * LOG2E)
        x = _mixer_out(o, h, x, band, pool_w_groups[l].astype(BF16), pool_scale[l][None],
                       wpa, wpp, wout, ln2_g[l][None], ln2_b[l][None])
        (x,) = _ffn_ln(x, wg2, wu2, wd2, ln3_g[l][None], ln3_b[l][None], "ffn2_ln3", False)
    return x.reshape(batch, seq, d)
```

```python
import functools
import math

import numpy as np
import jax
import jax.numpy as jnp
from jax import lax
from jax.experimental import pallas as pl
from jax.experimental.pallas import tpu as pltpu

D_MODEL = 2048
SEQ = 16384
HEAD_DIM = 128
N_Q_HEADS = 16
N_KV_HEADS = 4
Q_PER_KV = N_Q_HEADS // N_KV_HEADS
ATTN_WIDTH = N_Q_HEADS * HEAD_DIM
KV_WIDTH = N_KV_HEADS * HEAD_DIM
WINDOW = 128
BLOCK = 128
POOL_WINDOWS = (2, 4, 8, 16)
N_POOL_GROUPS = len(POOL_WINDOWS)
POOL_GROUP_DIM = 256
POOL_WIDTH = N_POOL_GROUPS * POOL_GROUP_DIM
ALPHA = 2.0 ** 0.25
LN_EPS = 1e-5
NEG_INF = -1e30
LOG2E = math.log2(math.e)

K_COL = ATTN_WIDTH // HEAD_DIM
V_COL = (ATTN_WIDTH + KV_WIDTH) // HEAD_DIM
POOL_OFF = ATTN_WIDTH + 2 * KV_WIDTH
GATE_A_OFF = POOL_OFF + POOL_WIDTH
GATE_B_OFF = GATE_A_OFF + D_MODEL

V7X_VMEM_LIMIT_BYTES = 56 * 1024 * 1024
BF16_TILE_ROWS = 16
POOL_HALO = BF16_TILE_ROWS

FFN_TM = 512
FFN_TF = 512
FFN_HEAD_TF = 256
FFN_NORM_SLAB = 64
FFN_X_PIECES = 2
FFN_X_SWITCH = (3, 7)
PROJ_TM = 2048
PROJ_TN = 1024
ATTN_TQ = 1024
MIX_TM = 256

F32 = jnp.float32
BF16 = jnp.bfloat16


def _layer_norm(y, g, b):
    mu = jnp.mean(y, axis=-1, keepdims=True)
    yc = y - mu
    var = jnp.mean(yc * yc, axis=-1, keepdims=True)
    return yc * lax.rsqrt(var + LN_EPS) * g + b


def _sigmoid(x):
    return 1.0 / (1.0 + jnp.exp(-x))


def _ffn_head_kernel(x_ref, wg_ref, wu_ref, wd_ref, g_ref, b_ref,
                     y_ref, yb_ref, wgb_ref, wub_ref, wdb_ref, xb_ref, acc_ref):
    j = pl.program_id(0)

    @pl.when(j == 0)
    def _():
        x = x_ref[...]
        xb_ref[...] = x.astype(BF16)
        acc_ref[...] = (2.0 * ALPHA) * x

    wg, wu, wd = (r[...].astype(BF16) for r in (wg_ref, wu_ref, wd_ref))
    wgb_ref[...] = wg
    wub_ref[...] = wu
    wdb_ref[...] = wd
    xb = xb_ref[...]
    gate = jnp.dot(xb, wg, preferred_element_type=F32)
    up = jnp.dot(xb, wu, preferred_element_type=F32)
    act = (gate * _sigmoid(gate) * up).astype(BF16)
    acc_ref[...] += jnp.dot(act, wd, preferred_element_type=F32)

    @pl.when(j == pl.num_programs(0) - 1)
    def _():
        y = _layer_norm(0.5 * acc_ref[...], g_ref[...], b_ref[...])
        y_ref[...] = y
        yb_ref[...] = y.astype(BF16)


def _ffn_head(x, wg, wu, wd, g, b):
    s, d = x.shape
    f = wg.shape[1]
    tm, tf = FFN_TM, FFN_HEAD_TF
    nj = f // tf
    assert nj * tf == f
    return pl.pallas_call(
        _ffn_head_kernel,
        grid=(nj,),
        in_specs=[
            pl.BlockSpec((tm, d), lambda j: (0, 0)),
            pl.BlockSpec((d, tf), lambda j: (0, j)),
            pl.BlockSpec((d, tf), lambda j: (0, j)),
            pl.BlockSpec((tf, d), lambda j: (j, 0)),
            pl.BlockSpec((1, d), lambda j: (0, 0)),
            pl.BlockSpec((1, d), lambda j: (0, 0)),
        ],
        out_specs=[
            pl.BlockSpec((tm, d), lambda j: (0, 0)),
            pl.BlockSpec((tm, d), lambda j: (0, 0)),
            pl.BlockSpec((d, tf), lambda j: (0, j)),
            pl.BlockSpec((d, tf), lambda j: (0, j)),
            pl.BlockSpec((tf, d), lambda j: (j, 0)),
        ],
        out_shape=[
            jax.ShapeDtypeStruct((s, d), F32),
            jax.ShapeDtypeStruct((s, d), BF16),
            jax.ShapeDtypeStruct(wg.shape, BF16),
            jax.ShapeDtypeStruct(wu.shape, BF16),
            jax.ShapeDtypeStruct(wd.shape, BF16),
        ],
        scratch_shapes=[pltpu.VMEM((tm, d), BF16), pltpu.VMEM((tm, d), F32)],
        compiler_params=pltpu.CompilerParams(
            dimension_semantics=("arbitrary",),
            vmem_limit_bytes=V7X_VMEM_LIMIT_BYTES),
        name="ffn1_head",
    )(x, wg, wu, wd, g, b)


def _ffn_ln_kernel(n_out, n_side, *refs):
    x_refs = refs[:FFN_X_PIECES]
    wg_ref, wu_ref, wd_ref, g_ref, b_ref = refs[FFN_X_PIECES:FFN_X_PIECES + 5]
    rest = refs[FFN_X_PIECES + 5:]
    side_in = rest[:n_side]
    rest = rest[n_side:]
    xb_ref, acc_ref, ypre_ref = rest[-3:]
    rest = rest[:-3]
    out_refs = rest[-(n_out + n_side):-n_side] if n_side else rest[-n_out:]
    side_out = rest[len(rest) - n_side:]
    i = pl.program_id(0)
    j = pl.program_id(1)
    n_tiles = pl.num_programs(0) - 1
    last_j = pl.num_programs(1) - 1
    slab = FFN_NORM_SLAB
    norm_steps = xb_ref.shape[0] // slab
    slabs_per_half = norm_steps // 2

    def down_proj(xb):
        gate = jnp.dot(xb, wg_ref[...], preferred_element_type=F32)
        up = jnp.dot(xb, wu_ref[...], preferred_element_type=F32)
        act = (gate * _sigmoid(gate) * up).astype(BF16)
        return jnp.dot(act, wd_ref[...], preferred_element_type=F32)

    def norm_slab():
        rows = pl.ds(pl.multiple_of(j * slab, slab), slab)
        half_rows = pl.ds(pl.multiple_of((j % slabs_per_half) * slab, slab), slab)
        y = _layer_norm(ypre_ref[rows, :], g_ref[...], b_ref[...])
        out_refs[0][half_rows, :] = y
        if n_out == 2:
            out_refs[1][half_rows, :] = y.astype(BF16)

    def first_chunk(with_norm):
        if with_norm:
            norm_slab()
        x = jnp.concatenate([r[...] for r in x_refs], axis=0)
        xb = x.astype(BF16)
        xb_ref[...] = xb
        acc_ref[...] = (2.0 * ALPHA) * x + down_proj(xb)

    def middle_chunk(with_norm):
        if with_norm:
            norm_slab()
        acc_ref[...] += down_proj(xb_ref[...])

    real_tile = i < n_tiles
    norm_due = (i > 0) & (j < norm_steps)
    middle = (j > 0) & (j < last_j)
    pl.when((j == 0) & (i == 0))(lambda: first_chunk(False))
    pl.when((j == 0) & (i > 0) & real_tile)(lambda: first_chunk(True))
    pl.when(middle & real_tile & norm_due)(lambda: middle_chunk(True))
    pl.when(middle & real_tile & jnp.logical_not(norm_due))(lambda: middle_chunk(False))
    pl.when(jnp.logical_not(real_tile) & norm_due)(norm_slab)

    def cast_side_blocks():
        for src, dst in zip(side_in, side_out):
            dst[...] = src[...].astype(BF16)

    @pl.when((j == last_j) & real_tile)
    def _():
        ypre_ref[...] = 0.5 * (acc_ref[...] + down_proj(xb_ref[...]))
        cast_side_blocks()

    if n_side:
        pl.when((j == last_j) & jnp.logical_not(real_tile))(cast_side_blocks)


def _ffn_ln(x, wg, wu, wd, g, b, name, with_bf16_copy, side_casts=(), head=None):
    s, d = x.shape
    f = wg.shape[1]
    tm, tf = FFN_TM, FFN_TF
    first = 0 if head is None else 1
    n, nj = s // tm - first, f // tf
    n_out = 2 if with_bf16_copy else 1
    head = () if head is None else tuple(head)
    assert len(head) in (0, n_out)
    norm_steps = tm // FFN_NORM_SLAB
    assert norm_steps < nj and norm_steps % 2 == 0 and FFN_NORM_SLAB % BF16_TILE_ROWS == 0

    def wcol(i, j):
        return jnp.where(i < n, j, nj - 1)

    side_specs, side_shapes = [], []
    for w in side_casts:
        n_blocks = n + first
        rows = w.shape[0] // n_blocks
        assert rows * n_blocks == w.shape[0] and rows % BF16_TILE_ROWS == 0
        side_specs.append(pl.BlockSpec((rows, w.shape[1]),
                                       lambda i, j: (jnp.minimum(i, n_blocks - 1), 0)))
        side_shapes.append(jax.ShapeDtypeStruct(w.shape, BF16))
    piece = tm // FFN_X_PIECES
    assert piece * FFN_X_PIECES == tm and len(FFN_X_SWITCH) == FFN_X_PIECES
    assert 0 < min(FFN_X_SWITCH) and max(FFN_X_SWITCH) < nj

    def x_spec(p):
        def index_map(i, j):
            tile = jnp.minimum(i + (j >= FFN_X_SWITCH[p]).astype(jnp.int32), n - 1) + first
            return (tile * FFN_X_PIECES + p, 0)
        return pl.BlockSpec((piece, d), index_map)

    half_steps = norm_steps // 2

    def out_index(i, j):
        done = (j >= half_steps).astype(jnp.int32) + (j >= 2 * half_steps).astype(jnp.int32)
        half = jnp.where(i == 0, 0, jnp.minimum(2 * (i - 1) + done, 2 * n - 1))
        return (half + 2 * first, 0)

    out_spec = pl.BlockSpec((tm // 2, d), out_index)
    n_in = FFN_X_PIECES + 5 + len(side_casts)
    return pl.pallas_call(
        functools.partial(_ffn_ln_kernel, n_out, len(side_casts)),
        grid=(n + 1, nj),
        in_specs=[x_spec(p) for p in range(FFN_X_PIECES)] + [
            pl.BlockSpec((d, tf), lambda i, j: (0, wcol(i, j))),
            pl.BlockSpec((d, tf), lambda i, j: (0, wcol(i, j))),
            pl.BlockSpec((tf, d), lambda i, j: (wcol(i, j), 0)),
            pl.BlockSpec((1, d), lambda i, j: (0, 0)),
            pl.BlockSpec((1, d), lambda i, j: (0, 0)),
        ] + side_specs + [pl.BlockSpec(memory_space=pl.ANY)] * len(head),
        input_output_aliases={n_in + k: k for k in range(len(head))},
        out_specs=[out_spec] * n_out + side_specs,
        out_shape=[jax.ShapeDtypeStruct((s, d), F32),
                   jax.ShapeDtypeStruct((s, d), BF16)][:n_out] + side_shapes,
        scratch_shapes=[pltpu.VMEM((tm, d), BF16), pltpu.VMEM((tm, d), F32),
                        pltpu.VMEM((tm, d), F32)],
        compiler_params=pltpu.CompilerParams(
            dimension_semantics=("arbitrary", "arbitrary"),
            vmem_limit_bytes=V7X_VMEM_LIMIT_BYTES),
        name=name,
    )(*([x] * FFN_X_PIECES), wg, wu, wd, g, b, *side_casts, *head)


def _proj_kernel(repeats, x_ref, w_ref, *rest):
    n_side = len(repeats)
    side_in = rest[:n_side]
    o_ref = rest[n_side]
    side_out = rest[n_side + 1:]
    step = pl.program_id(0) * pl.num_programs(1) + pl.program_id(1)

    o_ref[...] = jnp.dot(x_ref[...], w_ref[...], preferred_element_type=F32).astype(o_ref.dtype)

    for src, dst, rep in zip(side_in, side_out, repeats):
        if rep == 1:
            dst[...] = src[...].astype(BF16)
        else:
            @pl.when(step % rep == 0)
            def _(src=src, dst=dst):
                dst[...] = src[...].astype(BF16)


def _in_proj(xb, w, side_casts):
    s, d = xb.shape
    n = w.shape[1]
    tm, tn = PROJ_TM, PROJ_TN
    gi, gj = s // tm, n // tn
    steps = gi * gj

    def step_map(rep):
        return lambda i, j: ((i * gj + j) // rep, 0)

    side_specs, side_shapes, repeats = [], [], []
    for a in side_casts:
        n_blocks = steps
        while a.shape[0] % (BF16_TILE_ROWS * n_blocks):
            n_blocks //= 2
        assert n_blocks >= 1 and steps % n_blocks == 0
        repeats.append(steps // n_blocks)
        side_specs.append(pl.BlockSpec((a.shape[0] // n_blocks, a.shape[1]),
                                       step_map(repeats[-1])))
        side_shapes.append(jax.ShapeDtypeStruct(a.shape, BF16))
    return pl.pallas_call(
        functools.partial(_proj_kernel, tuple(repeats)),
        grid=(gi, gj),
        in_specs=[
            pl.BlockSpec((tm, d), lambda i, j: (i, 0)),
            pl.BlockSpec((d, tn), lambda i, j: (0, j)),
        ] + side_specs,
        out_specs=[pl.BlockSpec((tm, tn), lambda i, j: (i, j))] + side_specs,
        out_shape=[jax.ShapeDtypeStruct((s, n), BF16)] + side_shapes,
        compiler_params=pltpu.CompilerParams(
            dimension_semantics=("arbitrary", "arbitrary"),
            vmem_limit_bytes=V7X_VMEM_LIMIT_BYTES),
        name="mixer_in_proj",
    )(xb, w, *side_casts)


def _attn_kernel(sink_ref, q_ref, kp_ref, kc_ref, kn_ref, vp_ref, vc_ref, vn_ref, bias_ref,
                 o_ref, kcat_ref, vcat_ref):
    hk = pl.program_id(0)
    i = pl.program_id(1)
    tq = q_ref.shape[0]
    blocks_per_tile = tq // BLOCK

    kcat_ref[0:BLOCK, :] = kp_ref[...]
    kcat_ref[BLOCK:BLOCK + tq, :] = kc_ref[...]
    kcat_ref[BLOCK + tq:, :] = kn_ref[...]
    vcat_ref[0:BLOCK, 0:HEAD_DIM] = vp_ref[...]
    vcat_ref[BLOCK:BLOCK + tq, 0:HEAD_DIM] = vc_ref[...]
    vcat_ref[BLOCK + tq:, 0:HEAD_DIM] = vn_ref[...]
    vcat_ref[:, HEAD_DIM:] = jnp.ones((tq + 2 * BLOCK, HEAD_DIM), BF16)

    qk_scale = LOG2E / math.sqrt(HEAD_DIM)

    def bias_variant(b):
        if b == 0:
            return jnp.where(i == 0, 0, 1)
        if b == blocks_per_tile - 1:
            return jnp.where(i == pl.num_programs(1) - 1, 2, 1)
        return 1

    def scores(b, g):
        q = q_ref[b * BLOCK:(b + 1) * BLOCK, g * HEAD_DIM:(g + 1) * HEAD_DIM]
        kw = kcat_ref[b * BLOCK:(b + 3) * BLOCK, :]
        return lax.dot_general(q, kw, (((1,), (1,)), ((), ())), preferred_element_type=F32)

    def finish(b, g, s):
        sink = sink_ref[hk * Q_PER_KV + g]
        s = s * qk_scale + bias_ref[0, bias_variant(b), g]
        m = jnp.maximum(jnp.max(s, axis=-1, keepdims=True), sink)
        m_b = jnp.broadcast_to(m, (BLOCK, HEAD_DIM))
        p = jnp.exp2(s - jnp.concatenate([m_b] * 3, axis=-1))
        vw = vcat_ref[b * BLOCK:(b + 3) * BLOCK, :]
        ov = jnp.dot(p.astype(BF16), vw, preferred_element_type=F32)
        denom = ov[:, HEAD_DIM:] + jnp.exp2(sink - m_b)
        o_ref[b * BLOCK:(b + 1) * BLOCK, g * HEAD_DIM:(g + 1) * HEAD_DIM] = (
            ov[:, :HEAD_DIM] / denom).astype(o_ref.dtype)

    units = [(b, g) for b in range(blocks_per_tile) for g in range(Q_PER_KV)]
    s_next = scores(*units[0])
    for u, unit in enumerate(units):
        s_cur = s_next
        if u + 1 < len(units):
            s_next = scores(*units[u + 1])
        finish(*unit, s_cur)


def _attention(h, bias, sink_log2):
    s = h.shape[0]
    tq = ATTN_TQ
    r = tq // BLOCK
    nb = s // BLOCK
    assert nb >= 2
    gw = Q_PER_KV * HEAD_DIM

    def prev_map(col):
        return lambda hk, i: (jnp.maximum(i * r - 1, 0), col + hk)

    def cur_map(col):
        return lambda hk, i: (i, col + hk)

    def next_map(col):
        return lambda hk, i: (jnp.minimum((i + 1) * r, nb - 1), col + hk)

    return pl.pallas_call(
        _attn_kernel,
        grid=(N_KV_HEADS, s // tq),
        in_specs=[
            pl.BlockSpec(memory_space=pltpu.SMEM),
            pl.BlockSpec((tq, gw), lambda hk, i: (i, hk)),
            pl.BlockSpec((BLOCK, HEAD_DIM), prev_map(K_COL)),
            pl.BlockSpec((tq, HEAD_DIM), cur_map(K_COL)),
            pl.BlockSpec((BLOCK, HEAD_DIM), next_map(K_COL)),
            pl.BlockSpec((BLOCK, HEAD_DIM), prev_map(V_COL)),
            pl.BlockSpec((tq, HEAD_DIM), cur_map(V_COL)),
            pl.BlockSpec((BLOCK, HEAD_DIM), next_map(V_COL)),
            pl.BlockSpec((1, 3, Q_PER_KV, BLOCK, 3 * BLOCK), lambda hk, i: (hk, 0, 0, 0, 0)),
        ],
        out_specs=pl.BlockSpec((tq, gw), lambda hk, i: (i, hk)),
        out_shape=jax.ShapeDtypeStruct((s, ATTN_WIDTH), BF16),
        scratch_shapes=[
            pltpu.VMEM((tq + 2 * BLOCK, HEAD_DIM), BF16),
            pltpu.VMEM((tq + 2 * BLOCK, 2 * HEAD_DIM), BF16),
        ],
        compiler_params=pltpu.CompilerParams(
            dimension_semantics=("parallel", "parallel"),
            vmem_limit_bytes=V7X_VMEM_LIMIT_BYTES),
        name="banded_attention",
    )(sink_log2, h, h, h, h, h, h, h, bias)


def _attn_bias_table():
    a = np.arange(BLOCK)[:, None]
    c = np.arange(3 * BLOCK)[None, :]
    dist = np.abs(a + BLOCK - c)
    slopes = np.exp2(-8.0 * np.arange(1, N_Q_HEADS + 1, dtype=np.float32) / N_Q_HEADS)
    bias = (-slopes[:, None, None] * dist.astype(np.float32)[None]
            * np.float32(LOG2E))
    in_window = dist <= WINDOW
    variants = np.stack([in_window & (c >= BLOCK), in_window, in_window & (c < 2 * BLOCK)])
    table = np.where(variants[None], bias[:, None], np.float32(NEG_INF))
    table = table.reshape(N_KV_HEADS, Q_PER_KV, 3, BLOCK, 3 * BLOCK)
    return np.ascontiguousarray(table.transpose(0, 2, 1, 3, 4), dtype=np.float32)


def _pool_band_matrices(tm):
    t = np.arange(tm)[:, None]
    col = np.arange(tm + 2 * POOL_HALO)[None, :]
    pos = np.where(col < tm + POOL_HALO, col, col - tm - 2 * POOL_HALO)
    mats = []
    for w in POOL_WINDOWS:
        lo = t - w // 2
        hi = t + w - w // 2
        mats.append(((pos >= lo) & (pos < hi)).astype(np.float32))
    return np.stack(mats)


def _mixer_out_kernel(o_ref, pp_ref, pc_ref, pn_ref, ga_ref, gb_ref, x_ref, band_ref, wgrp_ref,
                      pscale_ref, wpa_ref, wpp_ref, wout_ref, g_ref, b_ref,
                      out_ref, pcat_ref, zpre_ref):
    i = pl.program_id(0)
    n_tiles = pl.num_programs(0) - 1
    tm = pc_ref.shape[0]
    seq = n_tiles * tm

    def norm_previous_tile():
        out_ref[...] = _layer_norm(zpre_ref[...], g_ref[...], b_ref[...])

    def mix_tile(with_norm):
        zeros_halo = jnp.zeros((POOL_HALO, POOL_WIDTH), BF16)
        pcat_ref[0:tm, :] = pc_ref[...]
        pcat_ref[tm:tm + POOL_HALO, :] = jnp.where(i == n_tiles - 1, zeros_halo, pn_ref[...])
        pcat_ref[tm + POOL_HALO:, :] = jnp.where(i == 0, zeros_halo, pp_ref[...])

        if with_norm:
            norm_previous_tile()
        y_a = jnp.dot(o_ref[...], wpa_ref[...], preferred_element_type=F32)

        t = i * tm + lax.broadcasted_iota(jnp.int32, (tm, 1), 0)
        groups = range(N_POOL_GROUPS)
        cols = [slice(g * POOL_GROUP_DIM, (g + 1) * POOL_GROUP_DIM) for g in groups]
        win_sums = [jnp.dot(band_ref[g], pcat_ref[:, cols[g]], preferred_element_type=F32)
                    for g in groups]
        diffs = []
        for g, w in enumerate(POOL_WINDOWS):
            lo = jnp.clip(t - w // 2, 0, seq)
            hi = jnp.clip(t + w - w // 2, 0, seq)
            mean = win_sums[g] / (hi - lo).astype(F32)
            diffs.append((mean - pc_ref[:, cols[g]].astype(F32)).astype(BF16))
        maps = [jnp.dot(diffs[g], wgrp_ref[g], preferred_element_type=F32) for g in groups]
        mixed = jnp.concatenate(
            [(maps[g] * pscale_ref[:, cols[g]]).astype(BF16) for g in groups], axis=-1)

        y_b = jnp.dot(mixed, wpp_ref[...], preferred_element_type=F32)
        merged = (_sigmoid(ga_ref[...].astype(F32)) * y_a
                  + _sigmoid(gb_ref[...].astype(F32)) * y_b)
        z = jnp.dot(merged.astype(BF16), wout_ref[...], preferred_element_type=F32)
        zpre_ref[...] = ALPHA * x_ref[...] + z

    pl.when(i == 0)(lambda: mix_tile(False))
    pl.when((i > 0) & (i < n_tiles))(lambda: mix_tile(True))
    pl.when(i == n_tiles)(norm_previous_tile)


def _mixer_out(o, h, x, band, wgrp, pscale, wpa, wpp, wout, g, b):
    s, d = x.shape
    tm = MIX_TM
    hr = tm // POOL_HALO
    n_halo = s // POOL_HALO
    pool_cb = POOL_OFF // POOL_WIDTH
    ga_cb = GATE_A_OFF // D_MODEL
    gb_cb = GATE_B_OFF // D_MODEL
    assert POOL_OFF % POOL_WIDTH == 0 and GATE_A_OFF % D_MODEL == 0 and GATE_B_OFF % D_MODEL == 0

    n = s // tm

    def const(shape):
        return pl.BlockSpec(shape, lambda i: (0,) * len(shape), pipeline_mode=pl.Buffered(1))

    def tile(i):
        return jnp.minimum(i, n - 1)

    return pl.pallas_call(
        _mixer_out_kernel,
        grid=(n + 1,),
        in_specs=[
            pl.BlockSpec((tm, ATTN_WIDTH), lambda i: (tile(i), 0)),
            pl.BlockSpec((POOL_HALO, POOL_WIDTH),
                         lambda i: (jnp.maximum(tile(i) * hr - 1, 0), pool_cb)),
            pl.BlockSpec((tm, POOL_WIDTH), lambda i: (tile(i), pool_cb)),
            pl.BlockSpec((POOL_HALO, POOL_WIDTH),
                         lambda i: (jnp.minimum((tile(i) + 1) * hr, n_halo - 1), pool_cb)),
            pl.BlockSpec((tm, D_MODEL), lambda i: (tile(i), ga_cb)),
            pl.BlockSpec((tm, D_MODEL), lambda i: (tile(i), gb_cb)),
            pl.BlockSpec((tm, d), lambda i: (tile(i), 0)),
            const(band.shape),
            const(wgrp.shape),
            const(pscale.shape),
            const(wpa.shape),
            const(wpp.shape),
            const(wout.shape),
            const(g.shape),
            const(b.shape),
        ],
        out_specs=pl.BlockSpec((tm, d), lambda i: (jnp.maximum(i - 1, 0), 0)),
        out_shape=jax.ShapeDtypeStruct((s, d), F32),
        scratch_shapes=[pltpu.VMEM((tm + 2 * POOL_HALO, POOL_WIDTH), BF16),
                        pltpu.VMEM((tm, d), F32)],
        compiler_params=pltpu.CompilerParams(
            dimension_semantics=("arbitrary",),
            vmem_limit_bytes=V7X_VMEM_LIMIT_BYTES),
        name="mixer_out_ln",
    )(o, h, h, h, h, h, x, band, wgrp, pscale, wpa, wpp, wout, g, b)


def kernel(x, ffn1_w_gate, ffn1_w_up, ffn1_w_down, ln1_g, ln1_b, w_in, attn_sink, pool_w_groups,
           pool_scale, w_proj_attn, w_proj_pool, w_out, ln2_g, ln2_b, ffn2_w_gate, ffn2_w_up,
           ffn2_w_down, ln3_g, ln3_b):
    batch, seq, d = x.shape
    assert (batch, seq, d) == (1, SEQ, D_MODEL)
    x = x.reshape(seq, d)

    bias = jnp.asarray(_attn_bias_table())
    band = jnp.asarray(_pool_band_matrices(MIX_TM), BF16)
    depth = ffn1_w_gate.shape[0]
    for l in range(depth):
        y_head, yb_head, wg1, wu1, wd1 = _ffn_head(
            x, ffn1_w_gate[l], ffn1_w_up[l], ffn1_w_down[l], ln1_g[l][None], ln1_b[l][None])
        x, xb, w_in_b = _ffn_ln(
            x, wg1, wu1, wd1, ln1_g[l][None], ln1_b[l][None], "ffn1_ln1", True,
            side_casts=(w_in[l],), head=(y_head, yb_head))
        h, wg2, wu2, wd2, wpa, wpp, wout = _in_proj(
            xb, w_in_b, side_casts=(ffn2_w_gate[l], ffn2_w_up[l], ffn2_w_down[l],
                                    w_proj_attn[l], w_proj_pool[l], w_out[l]))
        o = _attention(h, bias, attn_sink[l].astype(F32) * LOG2E)
        x = _mixer_out(o, h, x, band, pool_w_groups[l].astype(BF16), pool_scale[l][None],
                       wpa, wpp, wout, ln2_g[l][None], ln2_b[l][None])
        (x,) = _ffn_ln(x, wg2, wu2, wd2, ln3_g[l][None], ln3_b[l][None], "ffn2_ln3", False)
    return x.reshape(batch, seq, d)
```

```python
import functools
import math

import numpy as np
import jax
import jax.numpy as jnp
from jax import lax
from jax.experimental import pallas as pl
from jax.experimental.pallas import tpu as pltpu

D_MODEL = 2048
SEQ = 16384
HEAD_DIM = 128
N_Q_HEADS = 16
N_KV_HEADS = 4
Q_PER_KV = N_Q_HEADS // N_KV_HEADS
ATTN_WIDTH = N_Q_HEADS * HEAD_DIM
KV_WIDTH = N_KV_HEADS * HEAD_DIM
WINDOW = 128
BLOCK = 128
POOL_WINDOWS = (2, 4, 8, 16)
N_POOL_GROUPS = len(POOL_WINDOWS)
POOL_GROUP_DIM = 256
POOL_WIDTH = N_POOL_GROUPS * POOL_GROUP_DIM
ALPHA = 2.0 ** 0.25
LN_EPS = 1e-5
NEG_INF = -1e30
LOG2E = math.log2(math.e)

K_COL = ATTN_WIDTH // HEAD_DIM
V_COL = (ATTN_WIDTH + KV_WIDTH) // HEAD_DIM
POOL_OFF = ATTN_WIDTH + 2 * KV_WIDTH
GATE_A_OFF = POOL_OFF + POOL_WIDTH
GATE_B_OFF = GATE_A_OFF + D_MODEL

V7X_VMEM_LIMIT_BYTES = 56 * 1024 * 1024
V7X_VMEM_LIMIT_LARGE_TILE_BYTES = 60 * 1024 * 1024
BF16_TILE_ROWS = 16
POOL_HALO = BF16_TILE_ROWS

FFN_TM = 512
FFN_TF = 512
FFN_HEAD_TF = 256
FFN_NORM_SLAB = 64
FFN_X_PIECES = 2
FFN_X_SWITCH = (3, 7)
PROJ_TM = 2048
PROJ_TN = 1024
ATTN_TQ = 1024
MIX_TM = 256

F32 = jnp.float32
BF16 = jnp.bfloat16


def _layer_norm(y, g, b):
    mu = jnp.mean(y, axis=-1, keepdims=True)
    yc = y - mu
    var = jnp.mean(yc * yc, axis=-1, keepdims=True)
    return yc * lax.rsqrt(var + LN_EPS) * g + b


def _sigmoid(x):
    return 1.0 / (1.0 + jnp.exp(-x))


def _ffn_head_kernel(x_ref, wg_ref, wu_ref, wd_ref, g_ref, b_ref,
                     y_ref, yb_ref, wgb_ref, wub_ref, wdb_ref, xb_ref, acc_ref):
    j = pl.program_id(0)

    @pl.when(j == 0)
    def _():
        x = x_ref[...]
        xb_ref[...] = x.astype(BF16)
        acc_ref[...] = (2.0 * ALPHA) * x

    wg, wu, wd = (r[...].astype(BF16) for r in (wg_ref, wu_ref, wd_ref))
    wgb_ref[...] = wg
    wub_ref[...] = wu
    wdb_ref[...] = wd
    xb = xb_ref[...]
    gate = jnp.dot(xb, wg, preferred_element_type=F32)
    up = jnp.dot(xb, wu, preferred_element_type=F32)
    act = (gate * _sigmoid(gate) * up).astype(BF16)
    acc_ref[...] += jnp.dot(act, wd, preferred_element_type=F32)

    @pl.when(j == pl.num_programs(0) - 1)
    def _():
        y = _layer_norm(0.5 * acc_ref[...], g_ref[...], b_ref[...])
        y_ref[...] = y
        yb_ref[...] = y.astype(BF16)


def _ffn_head(x, wg, wu, wd, g, b):
    s, d = x.shape
    f = wg.shape[1]
    tm, tf = FFN_TM, FFN_HEAD_TF
    nj = f // tf
    assert nj * tf == f
    return pl.pallas_call(
        _ffn_head_kernel,
        grid=(nj,),
        in_specs=[
            pl.BlockSpec((tm, d), lambda j: (0, 0)),
            pl.BlockSpec((d, tf), lambda j: (0, j)),
            pl.BlockSpec((d, tf), lambda j: (0, j)),
            pl.BlockSpec((tf, d), lambda j: (j, 0)),
            pl.BlockSpec((1, d), lambda j: (0, 0)),
            pl.BlockSpec((1, d), lambda j: (0, 0)),
        ],
        out_specs=[
            pl.BlockSpec((tm, d), lambda j: (0, 0)),
            pl.BlockSpec((tm, d), lambda j: (0, 0)),
            pl.BlockSpec((d, tf), lambda j: (0, j)),
            pl.BlockSpec((d, tf), lambda j: (0, j)),
            pl.BlockSpec((tf, d), lambda j: (j, 0)),
        ],
        out_shape=[
            jax.ShapeDtypeStruct((s, d), F32),
            jax.ShapeDtypeStruct((s, d), BF16),
            jax.ShapeDtypeStruct(wg.shape, BF16),
            jax.ShapeDtypeStruct(wu.shape, BF16),
            jax.ShapeDtypeStruct(wd.shape, BF16),
        ],
        scratch_shapes=[pltpu.VMEM((tm, d), BF16), pltpu.VMEM((tm, d), F32)],
        compiler_params=pltpu.CompilerParams(
            dimension_semantics=("arbitrary",),
            vmem_limit_bytes=V7X_VMEM_LIMIT_BYTES),
        name="ffn1_head",
    )(x, wg, wu, wd, g, b)


def _ffn_ln_kernel(n_out, n_side, slab, out_parts, *refs):
    x_refs = refs[:FFN_X_PIECES]
    wg_ref, wu_ref, wd_ref, g_ref, b_ref = refs[FFN_X_PIECES:FFN_X_PIECES + 5]
    rest = refs[FFN_X_PIECES + 5:]
    side_in = rest[:n_side]
    rest = rest[n_side:]
    xb_ref, acc_ref, ypre_ref = rest[-3:]
    rest = rest[:-3]
    out_refs = rest[-(n_out + n_side):-n_side] if n_side else rest[-n_out:]
    side_out = rest[len(rest) - n_side:]
    i = pl.program_id(0)
    j = pl.program_id(1)
    n_tiles = pl.num_programs(0) - 1
    last_j = pl.num_programs(1) - 1
    norm_steps = xb_ref.shape[0] // slab
    slabs_per_half = norm_steps // out_parts

    def down_proj(xb):
        gate = jnp.dot(xb, wg_ref[...], preferred_element_type=F32)
        up = jnp.dot(xb, wu_ref[...], preferred_element_type=F32)
        act = (gate * _sigmoid(gate) * up).astype(BF16)
        return jnp.dot(act, wd_ref[...], preferred_element_type=F32)

    def norm_slab():
        rows = pl.ds(pl.multiple_of(j * slab, slab), slab)
        half_rows = pl.ds(pl.multiple_of((j % slabs_per_half) * slab, slab), slab)
        y = _layer_norm(ypre_ref[rows, :], g_ref[...], b_ref[...])
        out_refs[0][half_rows, :] = y
        if n_out == 2:
            out_refs[1][half_rows, :] = y.astype(BF16)

    def first_chunk(with_norm):
        if with_norm:
            norm_slab()
        x = jnp.concatenate([r[...] for r in x_refs], axis=0)
        xb = x.astype(BF16)
        xb_ref[...] = xb
        acc_ref[...] = (2.0 * ALPHA) * x + down_proj(xb)

    def middle_chunk(with_norm):
        if with_norm:
            norm_slab()
        acc_ref[...] += down_proj(xb_ref[...])

    real_tile = i < n_tiles
    norm_due = (i > 0) & (j < norm_steps)
    middle = (j > 0) & (j < last_j)
    pl.when((j == 0) & (i == 0))(lambda: first_chunk(False))
    pl.when((j == 0) & (i > 0) & real_tile)(lambda: first_chunk(True))
    pl.when(middle & real_tile & norm_due)(lambda: middle_chunk(True))
    pl.when(middle & real_tile & jnp.logical_not(norm_due))(lambda: middle_chunk(False))
    pl.when(jnp.logical_not(real_tile) & norm_due)(norm_slab)

    def cast_side_blocks():
        for src, dst in zip(side_in, side_out):
            dst[...] = src[...].astype(BF16)

    @pl.when((j == last_j) & real_tile)
    def _():
        ypre_ref[...] = 0.5 * (acc_ref[...] + down_proj(xb_ref[...]))
        cast_side_blocks()

    if n_side:
        pl.when((j == last_j) & jnp.logical_not(real_tile))(cast_side_blocks)


def _ffn_ln(x, wg, wu, wd, g, b, name, with_bf16_copy, side_casts=(), head=None,
            tm=FFN_TM, slab=FFN_NORM_SLAB, parts=2, vmem=V7X_VMEM_LIMIT_BYTES):
    s, d = x.shape
    f = wg.shape[1]
    tf = FFN_TF
    first = 0 if head is None else 1
    n, nj = s // tm - first, f // tf
    n_out = 2 if with_bf16_copy else 1
    head = () if head is None else tuple(head)
    assert len(head) in (0, n_out)
    norm_steps = tm // slab
    assert norm_steps < nj and norm_steps % parts == 0 and slab % BF16_TILE_ROWS == 0

    def wcol(i, j):
        return jnp.where(i < n, j, nj - 1)

    side_specs, side_shapes = [], []
    for w in side_casts:
        n_blocks = n + first
        rows = w.shape[0] // n_blocks
        assert rows * n_blocks == w.shape[0] and rows % BF16_TILE_ROWS == 0
        side_specs.append(pl.BlockSpec((rows, w.shape[1]),
                                       lambda i, j: (jnp.minimum(i, n_blocks - 1), 0)))
        side_shapes.append(jax.ShapeDtypeStruct(w.shape, BF16))
    piece = tm // FFN_X_PIECES
    assert piece * FFN_X_PIECES == tm and len(FFN_X_SWITCH) == FFN_X_PIECES
    assert 0 < min(FFN_X_SWITCH) and max(FFN_X_SWITCH) < nj

    def x_spec(p):
        def index_map(i, j):
            tile = jnp.minimum(i + (j >= FFN_X_SWITCH[p]).astype(jnp.int32), n - 1) + first
            return (tile * FFN_X_PIECES + p, 0)
        return pl.BlockSpec((piece, d), index_map)

    half_steps = norm_steps // parts

    def out_index(i, j):
        done = sum((j >= (p + 1) * half_steps).astype(jnp.int32) for p in range(parts))
        half = jnp.where(i == 0, 0, jnp.minimum(parts * (i - 1) + done, parts * n - 1))
        return (half + parts * first, 0)

    out_spec = pl.BlockSpec((tm // parts, d), out_index)
    n_in = FFN_X_PIECES + 5 + len(side_casts)
    return pl.pallas_call(
        functools.partial(_ffn_ln_kernel, n_out, len(side_casts), slab, parts),
        grid=(n + 1, nj),
        in_specs=[x_spec(p) for p in range(FFN_X_PIECES)] + [
            pl.BlockSpec((d, tf), lambda i, j: (0, wcol(i, j))),
            pl.BlockSpec((d, tf), lambda i, j: (0, wcol(i, j))),
            pl.BlockSpec((tf, d), lambda i, j: (wcol(i, j), 0)),
            pl.BlockSpec((1, d), lambda i, j: (0, 0)),
            pl.BlockSpec((1, d), lambda i, j: (0, 0)),
        ] + side_specs + [pl.BlockSpec(memory_space=pl.ANY)] * len(head),
        input_output_aliases={n_in + k: k for k in range(len(head))},
        out_specs=[out_spec] * n_out + side_specs,
        out_shape=[jax.ShapeDtypeStruct((s, d), F32),
                   jax.ShapeDtypeStruct((s, d), BF16)][:n_out] + side_shapes,
        scratch_shapes=[pltpu.VMEM((tm, d), BF16), pltpu.VMEM((tm, d), F32),
                        pltpu.VMEM((tm, d), F32)],
        compiler_params=pltpu.CompilerParams(
            dimension_semantics=("arbitrary", "arbitrary"),
            vmem_limit_bytes=vmem),
        name=name,
    )(*([x] * FFN_X_PIECES), wg, wu, wd, g, b, *side_casts, *head)


def _proj_kernel(repeats, x_ref, w_ref, *rest):
    n_side = len(repeats)
    side_in = rest[:n_side]
    o_ref = rest[n_side]
    side_out = rest[n_side + 1:]
    step = pl.program_id(0) * pl.num_programs(1) + pl.program_id(1)

    o_ref[...] = jnp.dot(x_ref[...], w_ref[...], preferred_element_type=F32).astype(o_ref.dtype)

    for src, dst, rep in zip(side_in, side_out, repeats):
        if rep == 1:
            dst[...] = src[...].astype(BF16)
        else:
            @pl.when(step % rep == 0)
            def _(src=src, dst=dst):
                dst[...] = src[...].astype(BF16)


def _in_proj(xb, w, side_casts):
    s, d = xb.shape
    n = w.shape[1]
    tm, tn = PROJ_TM, PROJ_TN
    gi, gj = s // tm, n // tn
    steps = gi * gj

    def step_map(rep):
        return lambda i, j: ((i * gj + j) // rep, 0)

    side_specs, side_shapes, repeats = [], [], []
    for a in side_casts:
        n_blocks = steps
        while a.shape[0] % (BF16_TILE_ROWS * n_blocks):
            n_blocks //= 2
        assert n_blocks >= 1 and steps % n_blocks == 0
        repeats.append(steps // n_blocks)
        side_specs.append(pl.BlockSpec((a.shape[0] // n_blocks, a.shape[1]),
                                       step_map(repeats[-1])))
        side_shapes.append(jax.ShapeDtypeStruct(a.shape, BF16))
    return pl.pallas_call(
        functools.partial(_proj_kernel, tuple(repeats)),
        grid=(gi, gj),
        in_specs=[
            pl.BlockSpec((tm, d), lambda i, j: (i, 0)),
            pl.BlockSpec((d, tn), lambda i, j: (0, j)),
        ] + side_specs,
        out_specs=[pl.BlockSpec((tm, tn), lambda i, j: (i, j))] + side_specs,
        out_shape=[jax.ShapeDtypeStruct((s, n), BF16)] + side_shapes,
        compiler_params=pltpu.CompilerParams(
            dimension_semantics=("arbitrary", "arbitrary"),
            vmem_limit_bytes=V7X_VMEM_LIMIT_BYTES),
        name="mixer_in_proj",
    )(xb, w, *side_casts)


def _attn_kernel(sink_ref, q_ref, kp_ref, kc_ref, kn_ref, vp_ref, vc_ref, vn_ref, bias_ref,
                 o_ref, kcat_ref, vcat_ref):
    hk = pl.program_id(0)
    i = pl.program_id(1)
    tq = q_ref.shape[0]
    blocks_per_tile = tq // BLOCK

    kcat_ref[0:BLOCK, :] = kp_ref[...]
    kcat_ref[BLOCK:BLOCK + tq, :] = kc_ref[...]
    kcat_ref[BLOCK + tq:, :] = kn_ref[...]
    vcat_ref[0:BLOCK, 0:HEAD_DIM] = vp_ref[...]
    vcat_ref[BLOCK:BLOCK + tq, 0:HEAD_DIM] = vc_ref[...]
    vcat_ref[BLOCK + tq:, 0:HEAD_DIM] = vn_ref[...]
    vcat_ref[:, HEAD_DIM:] = jnp.ones((tq + 2 * BLOCK, HEAD_DIM), BF16)

    qk_scale = LOG2E / math.sqrt(HEAD_DIM)

    def bias_variant(b):
        if b == 0:
            return jnp.where(i == 0, 0, 1)
        if b == blocks_per_tile - 1:
            return jnp.where(i == pl.num_programs(1) - 1, 2, 1)
        return 1

    def scores(b, g):
        q = q_ref[b * BLOCK:(b + 1) * BLOCK, g * HEAD_DIM:(g + 1) * HEAD_DIM]
        kw = kcat_ref[b * BLOCK:(b + 3) * BLOCK, :]
        return lax.dot_general(q, kw, (((1,), (1,)), ((), ())), preferred_element_type=F32)

    def finish(b, g, s):
        sink = sink_ref[hk * Q_PER_KV + g]
        s = s * qk_scale + bias_ref[0, bias_variant(b), g]
        m = jnp.maximum(jnp.max(s, axis=-1, keepdims=True), sink)
        m_b = jnp.broadcast_to(m, (BLOCK, HEAD_DIM))
        p = jnp.exp2(s - jnp.concatenate([m_b] * 3, axis=-1))
        vw = vcat_ref[b * BLOCK:(b + 3) * BLOCK, :]
        ov = jnp.dot(p.astype(BF16), vw, preferred_element_type=F32)
        denom = ov[:, HEAD_DIM:] + jnp.exp2(sink - m_b)
        o_ref[b * BLOCK:(b + 1) * BLOCK, g * HEAD_DIM:(g + 1) * HEAD_DIM] = (
            ov[:, :HEAD_DIM] / denom).astype(o_ref.dtype)

    units = [(b, g) for b in range(blocks_per_tile) for g in range(Q_PER_KV)]
    s_next = scores(*units[0])
    for u, unit in enumerate(units):
        s_cur = s_next
        if u + 1 < len(units):
            s_next = scores(*units[u + 1])
        finish(*unit, s_cur)


def _attention(h, bias, sink_log2):
    s = h.shape[0]
    tq = ATTN_TQ
    r = tq // BLOCK
    nb = s // BLOCK
    assert nb >= 2
    gw = Q_PER_KV * HEAD_DIM

    def prev_map(col):
        return lambda hk, i: (jnp.maximum(i * r - 1, 0), col + hk)

    def cur_map(col):
        return lambda hk, i: (i, col + hk)

    def next_map(col):
        return lambda hk, i: (jnp.minimum((i + 1) * r, nb - 1), col + hk)

    return pl.pallas_call(
        _attn_kernel,
        grid=(N_KV_HEADS, s // tq),
        in_specs=[
            pl.BlockSpec(memory_space=pltpu.SMEM),
            pl.BlockSpec((tq, gw), lambda hk, i: (i, hk)),
            pl.BlockSpec((BLOCK, HEAD_DIM), prev_map(K_COL)),
            pl.BlockSpec((tq, HEAD_DIM), cur_map(K_COL)),
            pl.BlockSpec((BLOCK, HEAD_DIM), next_map(K_COL)),
            pl.BlockSpec((BLOCK, HEAD_DIM), prev_map(V_COL)),
            pl.BlockSpec((tq, HEAD_DIM), cur_map(V_COL)),
            pl.BlockSpec((BLOCK, HEAD_DIM), next_map(V_COL)),
            pl.BlockSpec((1, 3, Q_PER_KV, BLOCK, 3 * BLOCK), lambda hk, i: (hk, 0, 0, 0, 0)),
        ],
        out_specs=pl.BlockSpec((tq, gw), lambda hk, i: (i, hk)),
        out_shape=jax.ShapeDtypeStruct((s, ATTN_WIDTH), BF16),
        scratch_shapes=[
            pltpu.VMEM((tq + 2 * BLOCK, HEAD_DIM), BF16),
            pltpu.VMEM((tq + 2 * BLOCK, 2 * HEAD_DIM), BF16),
        ],
        compiler_params=pltpu.CompilerParams(
            dimension_semantics=("parallel", "parallel"),
            vmem_limit_bytes=V7X_VMEM_LIMIT_BYTES),
        name="banded_attention",
    )(sink_log2, h, h, h, h, h, h, h, bias)


def _attn_bias_table():
    a = np.arange(BLOCK)[:, None]
    c = np.arange(3 * BLOCK)[None, :]
    dist = np.abs(a + BLOCK - c)
    slopes = np.exp2(-8.0 * np.arange(1, N_Q_HEADS + 1, dtype=np.float32) / N_Q_HEADS)
    bias = (-slopes[:, None, None] * dist.astype(np.float32)[None]
            * np.float32(LOG2E))
    in_window = dist <= WINDOW
    variants = np.stack([in_window & (c >= BLOCK), in_window, in_window & (c < 2 * BLOCK)])
    table = np.where(variants[None], bias[:, None], np.float32(NEG_INF))
    table = table.reshape(N_KV_HEADS, Q_PER_KV, 3, BLOCK, 3 * BLOCK)
    return np.ascontiguousarray(table.transpose(0, 2, 1, 3, 4), dtype=np.float32)


def _pool_band_matrices(tm):
    t = np.arange(tm)[:, None]
    col = np.arange(tm + 2 * POOL_HALO)[None, :]
    pos = np.where(col < tm + POOL_HALO, col, col - tm - 2 * POOL_HALO)
    mats = []
    for w in POOL_WINDOWS:
        lo = t - w // 2
        hi = t + w - w // 2
        mats.append(((pos >= lo) & (pos < hi)).astype(np.float32))
    return np.stack(mats)


def _mixer_out_kernel(o_ref, pp_ref, pc_ref, pn_ref, ga_ref, gb_ref, x_ref, band_ref, wgrp_ref,
                      pscale_ref, wpa_ref, wpp_ref, wout_ref, g_ref, b_ref,
                      out_ref, pcat_ref, zpre_ref):
    i = pl.program_id(0)
    n_tiles = pl.num_programs(0) - 1
    tm = pc_ref.shape[0]
    seq = n_tiles * tm

    def norm_previous_tile():
        out_ref[...] = _layer_norm(zpre_ref[...], g_ref[...], b_ref[...])

    def mix_tile(with_norm):
        zeros_halo = jnp.zeros((POOL_HALO, POOL_WIDTH), BF16)
        pcat_ref[0:tm, :] = pc_ref[...]
        pcat_ref[tm:tm + POOL_HALO, :] = jnp.where(i == n_tiles - 1, zeros_halo, pn_ref[...])
        pcat_ref[tm + POOL_HALO:, :] = jnp.where(i == 0, zeros_halo, pp_ref[...])

        if with_norm:
            norm_previous_tile()
        y_a = jnp.dot(o_ref[...], wpa_ref[...], preferred_element_type=F32)

        t = i * tm + lax.broadcasted_iota(jnp.int32, (tm, 1), 0)
        groups = range(N_POOL_GROUPS)
        cols = [slice(g * POOL_GROUP_DIM, (g + 1) * POOL_GROUP_DIM) for g in groups]
        win_sums = [jnp.dot(band_ref[g], pcat_ref[:, cols[g]], preferred_element_type=F32)
                    for g in groups]
        diffs = []
        for g, w in enumerate(POOL_WINDOWS):
            lo = jnp.clip(t - w // 2, 0, seq)
            hi = jnp.clip(t + w - w // 2, 0, seq)
            mean = win_sums[g] / (hi - lo).astype(F32)
            diffs.append((mean - pc_ref[:, cols[g]].astype(F32)).astype(BF16))
        maps = [jnp.dot(diffs[g], wgrp_ref[g], preferred_element_type=F32) for g in groups]
        mixed = jnp.concatenate(
            [(maps[g] * pscale_ref[:, cols[g]]).astype(BF16) for g in groups], axis=-1)

        y_b = jnp.dot(mixed, wpp_ref[...], preferred_element_type=F32)
        merged = (_sigmoid(ga_ref[...].astype(F32)) * y_a
                  + _sigmoid(gb_ref[...].astype(F32)) * y_b)
        z = jnp.dot(merged.astype(BF16), wout_ref[...], preferred_element_type=F32)
        zpre_ref[...] = ALPHA * x_ref[...] + z

    pl.when(i == 0)(lambda: mix_tile(False))
    pl.when((i > 0) & (i < n_tiles))(lambda: mix_tile(True))
    pl.when(i == n_tiles)(norm_previous_tile)


def _mixer_out(o, h, x, band, wgrp, pscale, wpa, wpp, wout, g, b):
    s, d = x.shape
    tm = MIX_TM
    hr = tm // POOL_HALO
    n_halo = s // POOL_HALO
    pool_cb = POOL_OFF // POOL_WIDTH
    ga_cb = GATE_A_OFF // D_MODEL
    gb_cb = GATE_B_OFF // D_MODEL
    assert POOL_OFF % POOL_WIDTH == 0 and GATE_A_OFF % D_MODEL == 0 and GATE_B_OFF % D_MODEL == 0

    n = s // tm

    def const(shape):
        return pl.BlockSpec(shape, lambda i: (0,) * len(shape), pipeline_mode=pl.Buffered(1))

    def tile(i):
        return jnp.minimum(i, n - 1)

    return pl.pallas_call(
        _mixer_out_kernel,
        grid=(n + 1,),
        in_specs=[
            pl.BlockSpec((tm, ATTN_WIDTH), lambda i: (tile(i), 0)),
            pl.BlockSpec((POOL_HALO, POOL_WIDTH),
                         lambda i: (jnp.maximum(tile(i) * hr - 1, 0), pool_cb)),
            pl.BlockSpec((tm, POOL_WIDTH), lambda i: (tile(i), pool_cb)),
            pl.BlockSpec((POOL_HALO, POOL_WIDTH),
                         lambda i: (jnp.minimum((tile(i) + 1) * hr, n_halo - 1), pool_cb)),
            pl.BlockSpec((tm, D_MODEL), lambda i: (tile(i), ga_cb)),
            pl.BlockSpec((tm, D_MODEL), lambda i: (tile(i), gb_cb)),
            pl.BlockSpec((tm, d), lambda i: (tile(i), 0)),
            const(band.shape),
            const(wgrp.shape),
            const(pscale.shape),
            const(wpa.shape),
            const(wpp.shape),
            const(wout.shape),
            const(g.shape),
            const(b.shape),
        ],
        out_specs=pl.BlockSpec((tm, d), lambda i: (jnp.maximum(i - 1, 0), 0)),
        out_shape=jax.ShapeDtypeStruct((s, d), F32),
        scratch_shapes=[pltpu.VMEM((tm + 2 * POOL_HALO, POOL_WIDTH), BF16),
                        pltpu.VMEM((tm, d), F32)],
        compiler_params=pltpu.CompilerParams(
            dimension_semantics=("arbitrary",),
            vmem_limit_bytes=V7X_VMEM_LIMIT_BYTES),
        name="mixer_out_ln",
    )(o, h, h, h, h, h, x, band, wgrp, pscale, wpa, wpp, wout, g, b)


def kernel(x, ffn1_w_gate, ffn1_w_up, ffn1_w_down, ln1_g, ln1_b, w_in, attn_sink, pool_w_groups,
           pool_scale, w_proj_attn, w_proj_pool, w_out, ln2_g, ln2_b, ffn2_w_gate, ffn2_w_up,
           ffn2_w_down, ln3_g, ln3_b):
    batch, seq, d = x.shape
    assert (batch, seq, d) == (1, SEQ, D_MODEL)
    x = x.reshape(seq, d)

    bias = jnp.asarray(_attn_bias_table())
    band = jnp.asarray(_pool_band_matrices(MIX_TM), BF16)
    depth = ffn1_w_gate.shape[0]
    for l in range(depth):
        y_head, yb_head, wg1, wu1, wd1 = _ffn_head(
            x, ffn1_w_gate[l], ffn1_w_up[l], ffn1_w_down[l], ln1_g[l][None], ln1_b[l][None])
        x, xb, w_in_b = _ffn_ln(
            x, wg1, wu1, wd1, ln1_g[l][None], ln1_b[l][None], "ffn1_ln1", True,
            side_casts=(w_in[l],), head=(y_head, yb_head))
        h, wg2, wu2, wd2, wpa, wpp, wout = _in_proj(
            xb, w_in_b, side_casts=(ffn2_w_gate[l], ffn2_w_up[l], ffn2_w_down[l],
                                    w_proj_attn[l], w_proj_pool[l], w_out[l]))
        o = _attention(h, bias, attn_sink[l].astype(F32) * LOG2E)
        x = _mixer_out(o, h, x, band, pool_w_groups[l].astype(BF16), pool_scale[l][None],
                       wpa, wpp, wout, ln2_g[l][None], ln2_b[l][None])
        (x,) = _ffn_ln(x, wg2, wu2, wd2, ln3_g[l][None], ln3_b[l][None], "ffn2_ln3", False,
                       tm=1024, slab=128, parts=4, vmem=V7X_VMEM_LIMIT_LARGE_TILE_BYTES)
    return x.reshape(batch, seq, d)
```

```python
import functools
import math

import numpy as np
import jax
import jax.numpy as jnp
from jax import lax
from jax.experimental import pallas as pl
from jax.experimental.pallas import tpu as pltpu

D_MODEL = 2048
SEQ = 16384
HEAD_DIM = 128
N_Q_HEADS = 16
N_KV_HEADS = 4
Q_PER_KV = N_Q_HEADS // N_KV_HEADS
ATTN_WIDTH = N_Q_HEADS * HEAD_DIM
KV_WIDTH = N_KV_HEADS * HEAD_DIM
WINDOW = 128
BLOCK = 128
POOL_WINDOWS = (2, 4, 8, 16)
N_POOL_GROUPS = len(POOL_WINDOWS)
POOL_GROUP_DIM = 256
POOL_WIDTH = N_POOL_GROUPS * POOL_GROUP_DIM
ALPHA = 2.0 ** 0.25
LN_EPS = 1e-5
NEG_INF = -1e30
LOG2E = math.log2(math.e)

K_COL = ATTN_WIDTH // HEAD_DIM
V_COL = (ATTN_WIDTH + KV_WIDTH) // HEAD_DIM
POOL_OFF = ATTN_WIDTH + 2 * KV_WIDTH
GATE_A_OFF = POOL_OFF + POOL_WIDTH
GATE_B_OFF = GATE_A_OFF + D_MODEL

V7X_VMEM_LIMIT_BYTES = 56 * 1024 * 1024
V7X_VMEM_LIMIT_LARGE_TILE_BYTES = 60 * 1024 * 1024
BF16_TILE_ROWS = 16
POOL_HALO = BF16_TILE_ROWS

FFN_TM = 512
FFN_TF = 512
FFN_HEAD_TF = 256
FFN_NORM_SLAB = 64
FFN_X_PIECES = 2
FFN_X_SWITCH = (3, 7)
PROJ_TM = 2048
PROJ_TN = 1024
ATTN_TQ = 2048
MIX_TM = 256

F32 = jnp.float32
BF16 = jnp.bfloat16


def _layer_norm(y, g, b):
    mu = jnp.mean(y, axis=-1, keepdims=True)
    yc = y - mu
    var = jnp.mean(yc * yc, axis=-1, keepdims=True)
    return yc * lax.rsqrt(var + LN_EPS) * g + b


def _sigmoid(x):
    return 1.0 / (1.0 + jnp.exp(-x))


def _ffn_head_kernel(x_ref, wg_ref, wu_ref, wd_ref, g_ref, b_ref,
                     y_ref, yb_ref, wgb_ref, wub_ref, wdb_ref, xb_ref, acc_ref):
    j = pl.program_id(0)

    @pl.when(j == 0)
    def _():
        x = x_ref[...]
        xb_ref[...] = x.astype(BF16)
        acc_ref[...] = (2.0 * ALPHA) * x

    wg, wu, wd = (r[...].astype(BF16) for r in (wg_ref, wu_ref, wd_ref))
    wgb_ref[...] = wg
    wub_ref[...] = wu
    wdb_ref[...] = wd
    xb = xb_ref[...]
    gate = jnp.dot(xb, wg, preferred_element_type=F32)
    up = jnp.dot(xb, wu, preferred_element_type=F32)
    act = (gate * _sigmoid(gate) * up).astype(BF16)
    acc_ref[...] += jnp.dot(act, wd, preferred_element_type=F32)

    @pl.when(j == pl.num_programs(0) - 1)
    def _():
        y = _layer_norm(0.5 * acc_ref[...], g_ref[...], b_ref[...])
        y_ref[...] = y
        yb_ref[...] = y.astype(BF16)


def _ffn_head(x, wg, wu, wd, g, b):
    s, d = x.shape
    f = wg.shape[1]
    tm, tf = FFN_TM, FFN_HEAD_TF
    nj = f // tf
    assert nj * tf == f
    return pl.pallas_call(
        _ffn_head_kernel,
        grid=(nj,),
        in_specs=[
            pl.BlockSpec((tm, d), lambda j: (0, 0)),
            pl.BlockSpec((d, tf), lambda j: (0, j)),
            pl.BlockSpec((d, tf), lambda j: (0, j)),
            pl.BlockSpec((tf, d), lambda j: (j, 0)),
            pl.BlockSpec((1, d), lambda j: (0, 0)),
            pl.BlockSpec((1, d), lambda j: (0, 0)),
        ],
        out_specs=[
            pl.BlockSpec((tm, d), lambda j: (0, 0)),
            pl.BlockSpec((tm, d), lambda j: (0, 0)),
            pl.BlockSpec((d, tf), lambda j: (0, j)),
            pl.BlockSpec((d, tf), lambda j: (0, j)),
            pl.BlockSpec((tf, d), lambda j: (j, 0)),
        ],
        out_shape=[
            jax.ShapeDtypeStruct((s, d), F32),
            jax.ShapeDtypeStruct((s, d), BF16),
            jax.ShapeDtypeStruct(wg.shape, BF16),
            jax.ShapeDtypeStruct(wu.shape, BF16),
            jax.ShapeDtypeStruct(wd.shape, BF16),
        ],
        scratch_shapes=[pltpu.VMEM((tm, d), BF16), pltpu.VMEM((tm, d), F32)],
        compiler_params=pltpu.CompilerParams(
            dimension_semantics=("arbitrary",),
            vmem_limit_bytes=V7X_VMEM_LIMIT_BYTES),
        name="ffn1_head",
    )(x, wg, wu, wd, g, b)


def _ffn_ln_kernel(n_out, n_side, slab, out_parts, *refs):
    x_refs = refs[:FFN_X_PIECES]
    wg_ref, wu_ref, wd_ref, g_ref, b_ref = refs[FFN_X_PIECES:FFN_X_PIECES + 5]
    rest = refs[FFN_X_PIECES + 5:]
    side_in = rest[:n_side]
    rest = rest[n_side:]
    xb_ref, acc_ref, ypre_ref = rest[-3:]
    rest = rest[:-3]
    out_refs = rest[-(n_out + n_side):-n_side] if n_side else rest[-n_out:]
    side_out = rest[len(rest) - n_side:]
    i = pl.program_id(0)
    j = pl.program_id(1)
    n_tiles = pl.num_programs(0) - 1
    last_j = pl.num_programs(1) - 1
    norm_steps = xb_ref.shape[0] // slab
    slabs_per_half = norm_steps // out_parts

    def down_proj(xb):
        gate = jnp.dot(xb, wg_ref[...], preferred_element_type=F32)
        up = jnp.dot(xb, wu_ref[...], preferred_element_type=F32)
        act = (gate * _sigmoid(gate) * up).astype(BF16)
        return jnp.dot(act, wd_ref[...], preferred_element_type=F32)

    def norm_slab():
        rows = pl.ds(pl.multiple_of(j * slab, slab), slab)
        half_rows = pl.ds(pl.multiple_of((j % slabs_per_half) * slab, slab), slab)
        y = _layer_norm(ypre_ref[rows, :], g_ref[...], b_ref[...])
        out_refs[0][half_rows, :] = y
        if n_out == 2:
            out_refs[1][half_rows, :] = y.astype(BF16)

    def first_chunk(with_norm):
        if with_norm:
            norm_slab()
        x = jnp.concatenate([r[...] for r in x_refs], axis=0)
        xb = x.astype(BF16)
        xb_ref[...] = xb
        acc_ref[...] = (2.0 * ALPHA) * x + down_proj(xb)

    def middle_chunk(with_norm):
        if with_norm:
            norm_slab()
        acc_ref[...] += down_proj(xb_ref[...])

    real_tile = i < n_tiles
    norm_due = (i > 0) & (j < norm_steps)
    middle = (j > 0) & (j < last_j)
    pl.when((j == 0) & (i == 0))(lambda: first_chunk(False))
    pl.when((j == 0) & (i > 0) & real_tile)(lambda: first_chunk(True))
    pl.when(middle & real_tile & norm_due)(lambda: middle_chunk(True))
    pl.when(middle & real_tile & jnp.logical_not(norm_due))(lambda: middle_chunk(False))
    pl.when(jnp.logical_not(real_tile) & norm_due)(norm_slab)

    def cast_side_blocks():
        for src, dst in zip(side_in, side_out):
            dst[...] = src[...].astype(BF16)

    @pl.when((j == last_j) & real_tile)
    def _():
        ypre_ref[...] = 0.5 * (acc_ref[...] + down_proj(xb_ref[...]))
        cast_side_blocks()

    if n_side:
        pl.when((j == last_j) & jnp.logical_not(real_tile))(cast_side_blocks)


def _ffn_ln(x, wg, wu, wd, g, b, name, with_bf16_copy, side_casts=(), head=None,
            tm=FFN_TM, slab=FFN_NORM_SLAB, parts=2, vmem=V7X_VMEM_LIMIT_BYTES):
    s, d = x.shape
    f = wg.shape[1]
    tf = FFN_TF
    first = 0 if head is None else 1
    n, nj = s // tm - first, f // tf
    n_out = 2 if with_bf16_copy else 1
    head = () if head is None else tuple(head)
    assert len(head) in (0, n_out)
    norm_steps = tm // slab
    assert norm_steps < nj and norm_steps % parts == 0 and slab % BF16_TILE_ROWS == 0

    def wcol(i, j):
        return jnp.where(i < n, j, nj - 1)

    side_specs, side_shapes = [], []
    for w in side_casts:
        n_blocks = n + first
        rows = w.shape[0] // n_blocks
        assert rows * n_blocks == w.shape[0] and rows % BF16_TILE_ROWS == 0
        side_specs.append(pl.BlockSpec((rows, w.shape[1]),
                                       lambda i, j: (jnp.minimum(i, n_blocks - 1), 0)))
        side_shapes.append(jax.ShapeDtypeStruct(w.shape, BF16))
    piece = tm // FFN_X_PIECES
    assert piece * FFN_X_PIECES == tm and len(FFN_X_SWITCH) == FFN_X_PIECES
    assert 0 < min(FFN_X_SWITCH) and max(FFN_X_SWITCH) < nj

    def x_spec(p):
        def index_map(i, j):
            tile = jnp.minimum(i + (j >= FFN_X_SWITCH[p]).astype(jnp.int32), n - 1) + first
            return (tile * FFN_X_PIECES + p, 0)
        return pl.BlockSpec((piece, d), index_map)

    half_steps = norm_steps // parts

    def out_index(i, j):
        done = sum((j >= (p + 1) * half_steps).astype(jnp.int32) for p in range(parts))
        half = jnp.where(i == 0, 0, jnp.minimum(parts * (i - 1) + done, parts * n - 1))
        return (half + parts * first, 0)

    out_spec = pl.BlockSpec((tm // parts, d), out_index)
    n_in = FFN_X_PIECES + 5 + len(side_casts)
    return pl.pallas_call(
        functools.partial(_ffn_ln_kernel, n_out, len(side_casts), slab, parts),
        grid=(n + 1, nj),
        in_specs=[x_spec(p) for p in range(FFN_X_PIECES)] + [
            pl.BlockSpec((d, tf), lambda i, j: (0, wcol(i, j))),
            pl.BlockSpec((d, tf), lambda i, j: (0, wcol(i, j))),
            pl.BlockSpec((tf, d), lambda i, j: (wcol(i, j), 0)),
            pl.BlockSpec((1, d), lambda i, j: (0, 0)),
            pl.BlockSpec((1, d), lambda i, j: (0, 0)),
        ] + side_specs + [pl.BlockSpec(memory_space=pl.ANY)] * len(head),
        input_output_aliases={n_in + k: k for k in range(len(head))},
        out_specs=[out_spec] * n_out + side_specs,
        out_shape=[jax.ShapeDtypeStruct((s, d), F32),
                   jax.ShapeDtypeStruct((s, d), BF16)][:n_out] + side_shapes,
        scratch_shapes=[pltpu.VMEM((tm, d), BF16), pltpu.VMEM((tm, d), F32),
                        pltpu.VMEM((tm, d), F32)],
        compiler_params=pltpu.CompilerParams(
            dimension_semantics=("arbitrary", "arbitrary"),
            vmem_limit_bytes=vmem),
        name=name,
    )(*([x] * FFN_X_PIECES), wg, wu, wd, g, b, *side_casts, *head)


def _proj_kernel(repeats, x_ref, w_ref, *rest):
    n_side = len(repeats)
    side_in = rest[:n_side]
    o_ref = rest[n_side]
    side_out = rest[n_side + 1:]
    step = pl.program_id(0) * pl.num_programs(1) + pl.program_id(1)

    o_ref[...] = jnp.dot(x_ref[...], w_ref[...], preferred_element_type=F32).astype(o_ref.dtype)

    for src, dst, rep in zip(side_in, side_out, repeats):
        if rep == 1:
            dst[...] = src[...].astype(BF16)
        else:
            @pl.when(step % rep == 0)
            def _(src=src, dst=dst):
                dst[...] = src[...].astype(BF16)


def _in_proj(xb, w, side_casts):
    s, d = xb.shape
    n = w.shape[1]
    tm, tn = PROJ_TM, PROJ_TN
    gi, gj = s // tm, n // tn
    steps = gi * gj

    def step_map(rep):
        return lambda i, j: ((i * gj + j) // rep, 0)

    side_specs, side_shapes, repeats = [], [], []
    for a in side_casts:
        n_blocks = steps
        while a.shape[0] % (BF16_TILE_ROWS * n_blocks):
            n_blocks //= 2
        assert n_blocks >= 1 and steps % n_blocks == 0
        repeats.append(steps // n_blocks)
        side_specs.append(pl.BlockSpec((a.shape[0] // n_blocks, a.shape[1]),
                                       step_map(repeats[-1])))
        side_shapes.append(jax.ShapeDtypeStruct(a.shape, BF16))
    return pl.pallas_call(
        functools.partial(_proj_kernel, tuple(repeats)),
        grid=(gi, gj),
        in_specs=[
            pl.BlockSpec((tm, d), lambda i, j: (i, 0)),
            pl.BlockSpec((d, tn), lambda i, j: (0, j)),
        ] + side_specs,
        out_specs=[pl.BlockSpec((tm, tn), lambda i, j: (i, j))] + side_specs,
        out_shape=[jax.ShapeDtypeStruct((s, n), BF16)] + side_shapes,
        compiler_params=pltpu.CompilerParams(
            dimension_semantics=("arbitrary", "arbitrary"),
            vmem_limit_bytes=V7X_VMEM_LIMIT_BYTES),
        name="mixer_in_proj",
    )(xb, w, *side_casts)


def _attn_kernel(sink_ref, q_ref, kp_ref, kc_ref, kn_ref, vp_ref, vc_ref, vn_ref, bias_ref,
                 o_ref, kcat_ref, vcat_ref):
    hk = pl.program_id(0)
    i = pl.program_id(1)
    tq = q_ref.shape[0]
    blocks_per_tile = tq // BLOCK

    kcat_ref[0:BLOCK, :] = kp_ref[...]
    kcat_ref[BLOCK:BLOCK + tq, :] = kc_ref[...]
    kcat_ref[BLOCK + tq:, :] = kn_ref[...]
    vcat_ref[0:BLOCK, 0:HEAD_DIM] = vp_ref[...]
    vcat_ref[BLOCK:BLOCK + tq, 0:HEAD_DIM] = vc_ref[...]
    vcat_ref[BLOCK + tq:, 0:HEAD_DIM] = vn_ref[...]
    vcat_ref[:, HEAD_DIM:] = jnp.ones((tq + 2 * BLOCK, HEAD_DIM), BF16)

    qk_scale = LOG2E / math.sqrt(HEAD_DIM)

    def bias_variant(b):
        if b == 0:
            return jnp.where(i == 0, 0, 1)
        if b == blocks_per_tile - 1:
            return jnp.where(i == pl.num_programs(1) - 1, 2, 1)
        return 1

    def scores(b, g):
        q = q_ref[b * BLOCK:(b + 1) * BLOCK, g * HEAD_DIM:(g + 1) * HEAD_DIM]
        kw = kcat_ref[b * BLOCK:(b + 3) * BLOCK, :]
        return lax.dot_general(q, kw, (((1,), (1,)), ((), ())), preferred_element_type=F32)

    def finish(b, g, s):
        sink = sink_ref[hk * Q_PER_KV + g]
        s = s * qk_scale + bias_ref[0, bias_variant(b), g]
        m = jnp.maximum(jnp.max(s, axis=-1, keepdims=True), sink)
        m_b = jnp.broadcast_to(m, (BLOCK, HEAD_DIM))
        p = jnp.exp2(s - jnp.concatenate([m_b] * 3, axis=-1))
        vw = vcat_ref[b * BLOCK:(b + 3) * BLOCK, :]
        ov = jnp.dot(p.astype(BF16), vw, preferred_element_type=F32)
        denom = ov[:, HEAD_DIM:] + jnp.exp2(sink - m_b)
        o_ref[b * BLOCK:(b + 1) * BLOCK, g * HEAD_DIM:(g + 1) * HEAD_DIM] = (
            ov[:, :HEAD_DIM] / denom).astype(o_ref.dtype)

    units = [(b, g) for b in range(blocks_per_tile) for g in range(Q_PER_KV)]
    s_next = scores(*units[0])
    for u, unit in enumerate(units):
        s_cur = s_next
        if u + 1 < len(units):
            s_next = scores(*units[u + 1])
        finish(*unit, s_cur)


def _attention(h, bias, sink_log2):
    s = h.shape[0]
    tq = ATTN_TQ
    r = tq // BLOCK
    nb = s // BLOCK
    assert nb >= 2
    gw = Q_PER_KV * HEAD_DIM

    def prev_map(col):
        return lambda hk, i: (jnp.maximum(i * r - 1, 0), col + hk)

    def cur_map(col):
        return lambda hk, i: (i, col + hk)

    def next_map(col):
        return lambda hk, i: (jnp.minimum((i + 1) * r, nb - 1), col + hk)

    return pl.pallas_call(
        _attn_kernel,
        grid=(N_KV_HEADS, s // tq),
        in_specs=[
            pl.BlockSpec(memory_space=pltpu.SMEM),
            pl.BlockSpec((tq, gw), lambda hk, i: (i, hk)),
            pl.BlockSpec((BLOCK, HEAD_DIM), prev_map(K_COL)),
            pl.BlockSpec((tq, HEAD_DIM), cur_map(K_COL)),
            pl.BlockSpec((BLOCK, HEAD_DIM), next_map(K_COL)),
            pl.BlockSpec((BLOCK, HEAD_DIM), prev_map(V_COL)),
            pl.BlockSpec((tq, HEAD_DIM), cur_map(V_COL)),
            pl.BlockSpec((BLOCK, HEAD_DIM), next_map(V_COL)),
            pl.BlockSpec((1, 3, Q_PER_KV, BLOCK, 3 * BLOCK), lambda hk, i: (hk, 0, 0, 0, 0)),
        ],
        out_specs=pl.BlockSpec((tq, gw), lambda hk, i: (i, hk)),
        out_shape=jax.ShapeDtypeStruct((s, ATTN_WIDTH), BF16),
        scratch_shapes=[
            pltpu.VMEM((tq + 2 * BLOCK, HEAD_DIM), BF16),
            pltpu.VMEM((tq + 2 * BLOCK, 2 * HEAD_DIM), BF16),
        ],
        compiler_params=pltpu.CompilerParams(
            dimension_semantics=("parallel", "parallel"),
            vmem_limit_bytes=V7X_VMEM_LIMIT_BYTES),
        name="banded_attention",
    )(sink_log2, h, h, h, h, h, h, h, bias)


def _attn_bias_table():
    a = np.arange(BLOCK)[:, None]
    c = np.arange(3 * BLOCK)[None, :]
    dist = np.abs(a + BLOCK - c)
    slopes = np.exp2(-8.0 * np.arange(1, N_Q_HEADS + 1, dtype=np.float32) / N_Q_HEADS)
    bias = (-slopes[:, None, None] * dist.astype(np.float32)[None]
            * np.float32(LOG2E))
    in_window = dist <= WINDOW
    variants = np.stack([in_window & (c >= BLOCK), in_window, in_window & (c < 2 * BLOCK)])
    table = np.where(variants[None], bias[:, None], np.float32(NEG_INF))
    table = table.reshape(N_KV_HEADS, Q_PER_KV, 3, BLOCK, 3 * BLOCK)
    return np.ascontiguousarray(table.transpose(0, 2, 1, 3, 4), dtype=np.float32)


def _pool_band_matrices(tm):
    t = np.arange(tm)[:, None]
    col = np.arange(tm + 2 * POOL_HALO)[None, :]
    pos = np.where(col < tm + POOL_HALO, col, col - tm - 2 * POOL_HALO)
    mats = []
    for w in POOL_WINDOWS:
        lo = t - w // 2
        hi = t + w - w // 2
        mats.append(((pos >= lo) & (pos < hi)).astype(np.float32))
    return np.stack(mats)


def _mixer_out_kernel(o_ref, pp_ref, pc_ref, pn_ref, ga_ref, gb_ref, x_ref, band_ref, wgrp_ref,
                      pscale_ref, wpa_ref, wpp_ref, wout_ref, g_ref, b_ref,
                      out_ref, pcat_ref, zpre_ref):
    i = pl.program_id(0)
    n_tiles = pl.num_programs(0) - 1
    tm = pc_ref.shape[0]
    seq = n_tiles * tm

    def norm_previous_tile():
        out_ref[...] = _layer_norm(zpre_ref[...], g_ref[...], b_ref[...])

    def mix_tile(with_norm):
        zeros_halo = jnp.zeros((POOL_HALO, POOL_WIDTH), BF16)
        pcat_ref[0:tm, :] = pc_ref[...]
        pcat_ref[tm:tm + POOL_HALO, :] = jnp.where(i == n_tiles - 1, zeros_halo, pn_ref[...])
        pcat_ref[tm + POOL_HALO:, :] = jnp.where(i == 0, zeros_halo, pp_ref[...])

        if with_norm:
            norm_previous_tile()
        y_a = jnp.dot(o_ref[...], wpa_ref[...], preferred_element_type=F32)

        t = i * tm + lax.broadcasted_iota(jnp.int32, (tm, 1), 0)
        groups = range(N_POOL_GROUPS)
        cols = [slice(g * POOL_GROUP_DIM, (g + 1) * POOL_GROUP_DIM) for g in groups]
        win_sums = [jnp.dot(band_ref[g], pcat_ref[:, cols[g]], preferred_element_type=F32)
                    for g in groups]
        diffs = []
        for g, w in enumerate(POOL_WINDOWS):
            lo = jnp.clip(t - w // 2, 0, seq)
            hi = jnp.clip(t + w - w // 2, 0, seq)
            mean = win_sums[g] / (hi - lo).astype(F32)
            diffs.append((mean - pc_ref[:, cols[g]].astype(F32)).astype(BF16))
        maps = [jnp.dot(diffs[g], wgrp_ref[g], preferred_element_type=F32) for g in groups]
        mixed = jnp.concatenate(
            [(maps[g] * pscale_ref[:, cols[g]]).astype(BF16) for g in groups], axis=-1)

        y_b = jnp.dot(mixed, wpp_ref[...], preferred_element_type=F32)
        merged = (_sigmoid(ga_ref[...].astype(F32)) * y_a
                  + _sigmoid(gb_ref[...].astype(F32)) * y_b)
        z = jnp.dot(merged.astype(BF16), wout_ref[...], preferred_element_type=F32)
        zpre_ref[...] = ALPHA * x_ref[...] + z

    pl.when(i == 0)(lambda: mix_tile(False))
    pl.when((i > 0) & (i < n_tiles))(lambda: mix_tile(True))
    pl.when(i == n_tiles)(norm_previous_tile)


def _mixer_out(o, h, x, band, wgrp, pscale, wpa, wpp, wout, g, b):
    s, d = x.shape
    tm = MIX_TM
    hr = tm // POOL_HALO
    n_halo = s // POOL_HALO
    pool_cb = POOL_OFF // POOL_WIDTH
    ga_cb = GATE_A_OFF // D_MODEL
    gb_cb = GATE_B_OFF // D_MODEL
    assert POOL_OFF % POOL_WIDTH == 0 and GATE_A_OFF % D_MODEL == 0 and GATE_B_OFF % D_MODEL == 0

    n = s // tm

    def const(shape):
        return pl.BlockSpec(shape, lambda i: (0,) * len(shape), pipeline_mode=pl.Buffered(1))

    def tile(i):
        return jnp.minimum(i, n - 1)

    return pl.pallas_call(
        _mixer_out_kernel,
        grid=(n + 1,),
        in_specs=[
            pl.BlockSpec((tm, ATTN_WIDTH), lambda i: (tile(i), 0)),
            pl.BlockSpec((POOL_HALO, POOL_WIDTH),
                         lambda i: (jnp.maximum(tile(i) * hr - 1, 0), pool_cb)),
            pl.BlockSpec((tm, POOL_WIDTH), lambda i: (tile(i), pool_cb)),
            pl.BlockSpec((POOL_HALO, POOL_WIDTH),
                         lambda i: (jnp.minimum((tile(i) + 1) * hr, n_halo - 1), pool_cb)),
            pl.BlockSpec((tm, D_MODEL), lambda i: (tile(i), ga_cb)),
            pl.BlockSpec((tm, D_MODEL), lambda i: (tile(i), gb_cb)),
            pl.BlockSpec((tm, d), lambda i: (tile(i), 0)),
            const(band.shape),
            const(wgrp.shape),
            const(pscale.shape),
            const(wpa.shape),
            const(wpp.shape),
            const(wout.shape),
            const(g.shape),
            const(b.shape),
        ],
        out_specs=pl.BlockSpec((tm, d), lambda i: (jnp.maximum(i - 1, 0), 0)),
        out_shape=jax.ShapeDtypeStruct((s, d), F32),
        scratch_shapes=[pltpu.VMEM((tm + 2 * POOL_HALO, POOL_WIDTH), BF16),
                        pltpu.VMEM((tm, d), F32)],
        compiler_params=pltpu.CompilerParams(
            dimension_semantics=("arbitrary",),
            vmem_limit_bytes=V7X_VMEM_LIMIT_BYTES),
        name="mixer_out_ln",
    )(o, h, h, h, h, h, x, band, wgrp, pscale, wpa, wpp, wout, g, b)


def kernel(x, ffn1_w_gate, ffn1_w_up, ffn1_w_down, ln1_g, ln1_b, w_in, attn_sink, pool_w_groups,
           pool_scale, w_proj_attn, w_proj_pool, w_out, ln2_g, ln2_b, ffn2_w_gate, ffn2_w_up,
           ffn2_w_down, ln3_g, ln3_b):
    batch, seq, d = x.shape
    assert (batch, seq, d) == (1, SEQ, D_MODEL)
    x = x.reshape(seq, d)

    bias = jnp.asarray(_attn_bias_table())
    band = jnp.asarray(_pool_band_matrices(MIX_TM), BF16)
    depth = ffn1_w_gate.shape[0]
    for l in range(depth):
        y_head, yb_head, wg1, wu1, wd1 = _ffn_head(
            x, ffn1_w_gate[l], ffn1_w_up[l], ffn1_w_down[l], ln1_g[l][None], ln1_b[l][None])
        x, xb, w_in_b = _ffn_ln(
            x, wg1, wu1, wd1, ln1_g[l][None], ln1_b[l][None], "ffn1_ln1", True,
            side_casts=(w_in[l],), head=(y_head, yb_head))
        h, wg2, wu2, wd2, wpa, wpp, wout = _in_proj(
            xb, w_in_b, side_casts=(ffn2_w_gate[l], ffn2_w_up[l], ffn2_w_down[l],
                                    w_proj_attn[l], w_proj_pool[l], w_out[l]))
        o = _attention(h, bias, attn_sink[l].astype(F32) * LOG2E)
        x = _mixer_out(o, h, x, band, pool_w_groups[l].astype(BF16), pool_scale[l][None],
                       wpa, wpp, wout, ln2_g[l][None], ln2_b[l][None])
        (x,) = _ffn_ln(x, wg2, wu2, wd2, ln3_g[l][None], ln3_b[l][None], "ffn2_ln3", False,
                       tm=1024, slab=128, parts=4, vmem=V7X_VMEM_LIMIT_LARGE_TILE_BYTES)
    return x.reshape(batch, seq, d)
```

```python
import functools
import math

import numpy as np
import jax
import jax.numpy as jnp
from jax import lax
from jax.experimental import pallas as pl
from jax.experimental.pallas import tpu as pltpu

D_MODEL = 2048
SEQ = 16384
HEAD_DIM = 128
N_Q_HEADS = 16
N_KV_HEADS = 4
Q_PER_KV = N_Q_HEADS // N_KV_HEADS
ATTN_WIDTH = N_Q_HEADS * HEAD_DIM
KV_WIDTH = N_KV_HEADS * HEAD_DIM
WINDOW = 128
BLOCK = 128
POOL_WINDOWS = (2, 4, 8, 16)
N_POOL_GROUPS = len(POOL_WINDOWS)
POOL_GROUP_DIM = 256
POOL_WIDTH = N_POOL_GROUPS * POOL_GROUP_DIM
ALPHA = 2.0 ** 0.25
LN_EPS = 1e-5
NEG_INF = -1e30
LOG2E = math.log2(math.e)

K_COL = ATTN_WIDTH // HEAD_DIM
V_COL = (ATTN_WIDTH + KV_WIDTH) // HEAD_DIM
POOL_OFF = ATTN_WIDTH + 2 * KV_WIDTH
GATE_A_OFF = POOL_OFF + POOL_WIDTH
GATE_B_OFF = GATE_A_OFF + D_MODEL

V7X_VMEM_LIMIT_BYTES = 56 * 1024 * 1024
V7X_VMEM_LIMIT_LARGE_TILE_BYTES = 60 * 1024 * 1024
BF16_TILE_ROWS = 16
POOL_HALO = BF16_TILE_ROWS

FFN_TM = 512
FFN_TF = 512
FFN_HEAD_TF = 256
FFN_NORM_SLAB = 64
FFN_X_PIECES = 2
FFN_X_SWITCH = (3, 7)
PROJ_TM = 2048
PROJ_TN = 1024
ATTN_TQ = 4096
MIX_TM = 256

F32 = jnp.float32
BF16 = jnp.bfloat16


def _layer_norm(y, g, b):
    mu = jnp.mean(y, axis=-1, keepdims=True)
    yc = y - mu
    var = jnp.mean(yc * yc, axis=-1, keepdims=True)
    return yc * lax.rsqrt(var + LN_EPS) * g + b


def _sigmoid(x):
    return 1.0 / (1.0 + jnp.exp(-x))


def _ffn_head_kernel(x_ref, wg_ref, wu_ref, wd_ref, g_ref, b_ref,
                     y_ref, yb_ref, wgb_ref, wub_ref, wdb_ref, xb_ref, acc_ref):
    j = pl.program_id(0)

    @pl.when(j == 0)
    def _():
        x = x_ref[...]
        xb_ref[...] = x.astype(BF16)
        acc_ref[...] = (2.0 * ALPHA) * x

    wg, wu, wd = (r[...].astype(BF16) for r in (wg_ref, wu_ref, wd_ref))
    wgb_ref[...] = wg
    wub_ref[...] = wu
    wdb_ref[...] = wd
    xb = xb_ref[...]
    gate = jnp.dot(xb, wg, preferred_element_type=F32)
    up = jnp.dot(xb, wu, preferred_element_type=F32)
    act = (gate * _sigmoid(gate) * up).astype(BF16)
    acc_ref[...] += jnp.dot(act, wd, preferred_element_type=F32)

    @pl.when(j == pl.num_programs(0) - 1)
    def _():
        y = _layer_norm(0.5 * acc_ref[...], g_ref[...], b_ref[...])
        y_ref[...] = y
        yb_ref[...] = y.astype(BF16)


def _ffn_head(x, wg, wu, wd, g, b):
    s, d = x.shape
    f = wg.shape[1]
    tm, tf = FFN_TM, FFN_HEAD_TF
    nj = f // tf
    assert nj * tf == f
    return pl.pallas_call(
        _ffn_head_kernel,
        grid=(nj,),
        in_specs=[
            pl.BlockSpec((tm, d), lambda j: (0, 0)),
            pl.BlockSpec((d, tf), lambda j: (0, j)),
            pl.BlockSpec((d, tf), lambda j: (0, j)),
            pl.BlockSpec((tf, d), lambda j: (j, 0)),
            pl.BlockSpec((1, d), lambda j: (0, 0)),
            pl.BlockSpec((1, d), lambda j: (0, 0)),
        ],
        out_specs=[
            pl.BlockSpec((tm, d), lambda j: (0, 0)),
            pl.BlockSpec((tm, d), lambda j: (0, 0)),
            pl.BlockSpec((d, tf), lambda j: (0, j)),
            pl.BlockSpec((d, tf), lambda j: (0, j)),
            pl.BlockSpec((tf, d), lambda j: (j, 0)),
        ],
        out_shape=[
            jax.ShapeDtypeStruct((s, d), F32),
            jax.ShapeDtypeStruct((s, d), BF16),
            jax.ShapeDtypeStruct(wg.shape, BF16),
            jax.ShapeDtypeStruct(wu.shape, BF16),
            jax.ShapeDtypeStruct(wd.shape, BF16),
        ],
        scratch_shapes=[pltpu.VMEM((tm, d), BF16), pltpu.VMEM((tm, d), F32)],
        compiler_params=pltpu.CompilerParams(
            dimension_semantics=("arbitrary",),
            vmem_limit_bytes=V7X_VMEM_LIMIT_BYTES),
        name="ffn1_head",
    )(x, wg, wu, wd, g, b)


def _ffn_ln_kernel(n_out, n_side, slab, out_parts, *refs):
    x_refs = refs[:FFN_X_PIECES]
    wg_ref, wu_ref, wd_ref, g_ref, b_ref = refs[FFN_X_PIECES:FFN_X_PIECES + 5]
    rest = refs[FFN_X_PIECES + 5:]
    side_in = rest[:n_side]
    rest = rest[n_side:]
    xb_ref, acc_ref, ypre_ref = rest[-3:]
    rest = rest[:-3]
    out_refs = rest[-(n_out + n_side):-n_side] if n_side else rest[-n_out:]
    side_out = rest[len(rest) - n_side:]
    i = pl.program_id(0)
    j = pl.program_id(1)
    n_tiles = pl.num_programs(0) - 1
    last_j = pl.num_programs(1) - 1
    norm_steps = xb_ref.shape[0] // slab
    slabs_per_half = norm_steps // out_parts

    def down_proj(xb):
        gate = jnp.dot(xb, wg_ref[...], preferred_element_type=F32)
        up = jnp.dot(xb, wu_ref[...], preferred_element_type=F32)
        act = (gate * _sigmoid(gate) * up).astype(BF16)
        return jnp.dot(act, wd_ref[...], preferred_element_type=F32)

    def norm_slab():
        rows = pl.ds(pl.multiple_of(j * slab, slab), slab)
        half_rows = pl.ds(pl.multiple_of((j % slabs_per_half) * slab, slab), slab)
        y = _layer_norm(ypre_ref[rows, :], g_ref[...], b_ref[...])
        out_refs[0][half_rows, :] = y
        if n_out == 2:
            out_refs[1][half_rows, :] = y.astype(BF16)

    def first_chunk(with_norm):
        if with_norm:
            norm_slab()
        x = jnp.concatenate([r[...] for r in x_refs], axis=0)
        xb = x.astype(BF16)
        xb_ref[...] = xb
        acc_ref[...] = (2.0 * ALPHA) * x + down_proj(xb)

    def middle_chunk(with_norm):
        if with_norm:
            norm_slab()
        acc_ref[...] += down_proj(xb_ref[...])

    real_tile = i < n_tiles
    norm_due = (i > 0) & (j < norm_steps)
    middle = (j > 0) & (j < last_j)
    pl.when((j == 0) & (i == 0))(lambda: first_chunk(False))
    pl.when((j == 0) & (i > 0) & real_tile)(lambda: first_chunk(True))
    pl.when(middle & real_tile & norm_due)(lambda: middle_chunk(True))
    pl.when(middle & real_tile & jnp.logical_not(norm_due))(lambda: middle_chunk(False))
    pl.when(jnp.logical_not(real_tile) & norm_due)(norm_slab)

    def cast_side_blocks():
        for src, dst in zip(side_in, side_out):
            dst[...] = src[...].astype(BF16)

    @pl.when((j == last_j) & real_tile)
    def _():
        ypre_ref[...] = 0.5 * (acc_ref[...] + down_proj(xb_ref[...]))
        cast_side_blocks()

    if n_side:
        pl.when((j == last_j) & jnp.logical_not(real_tile))(cast_side_blocks)


def _ffn_ln(x, wg, wu, wd, g, b, name, with_bf16_copy, side_casts=(), head=None,
            tm=FFN_TM, slab=FFN_NORM_SLAB, parts=2, vmem=V7X_VMEM_LIMIT_BYTES):
    s, d = x.shape
    f = wg.shape[1]
    tf = FFN_TF
    first = 0 if head is None else 1
    n, nj = s // tm - first, f // tf
    n_out = 2 if with_bf16_copy else 1
    head = () if head is None else tuple(head)
    assert len(head) in (0, n_out)
    norm_steps = tm // slab
    assert norm_steps < nj and norm_steps % parts == 0 and slab % BF16_TILE_ROWS == 0

    def wcol(i, j):
        return jnp.where(i < n, j, nj - 1)

    side_specs, side_shapes = [], []
    for w in side_casts:
        n_blocks = n + first
        rows = w.shape[0] // n_blocks
        assert rows * n_blocks == w.shape[0] and rows % BF16_TILE_ROWS == 0
        side_specs.append(pl.BlockSpec((rows, w.shape[1]),
                                       lambda i, j: (jnp.minimum(i, n_blocks - 1), 0)))
        side_shapes.append(jax.ShapeDtypeStruct(w.shape, BF16))
    piece = tm // FFN_X_PIECES
    assert piece * FFN_X_PIECES == tm and len(FFN_X_SWITCH) == FFN_X_PIECES
    assert 0 < min(FFN_X_SWITCH) and max(FFN_X_SWITCH) < nj

    def x_spec(p):
        def index_map(i, j):
            tile = jnp.minimum(i + (j >= FFN_X_SWITCH[p]).astype(jnp.int32), n - 1) + first
            return (tile * FFN_X_PIECES + p, 0)
        return pl.BlockSpec((piece, d), index_map)

    half_steps = norm_steps // parts

    def out_index(i, j):
        done = sum((j >= (p + 1) * half_steps).astype(jnp.int32) for p in range(parts))
        half = jnp.where(i == 0, 0, jnp.minimum(parts * (i - 1) + done, parts * n - 1))
        return (half + parts * first, 0)

    out_spec = pl.BlockSpec((tm // parts, d), out_index)
    n_in = FFN_X_PIECES + 5 + len(side_casts)
    return pl.pallas_call(
        functools.partial(_ffn_ln_kernel, n_out, len(side_casts), slab, parts),
        grid=(n + 1, nj),
        in_specs=[x_spec(p) for p in range(FFN_X_PIECES)] + [
            pl.BlockSpec((d, tf), lambda i, j: (0, wcol(i, j))),
            pl.BlockSpec((d, tf), lambda i, j: (0, wcol(i, j))),
            pl.BlockSpec((tf, d), lambda i, j: (wcol(i, j), 0)),
            pl.BlockSpec((1, d), lambda i, j: (0, 0)),
            pl.BlockSpec((1, d), lambda i, j: (0, 0)),
        ] + side_specs + [pl.BlockSpec(memory_space=pl.ANY)] * len(head),
        input_output_aliases={n_in + k: k for k in range(len(head))},
        out_specs=[out_spec] * n_out + side_specs,
        out_shape=[jax.ShapeDtypeStruct((s, d), F32),
                   jax.ShapeDtypeStruct((s, d), BF16)][:n_out] + side_shapes,
        scratch_shapes=[pltpu.VMEM((tm, d), BF16), pltpu.VMEM((tm, d), F32),
                        pltpu.VMEM((tm, d), F32)],
        compiler_params=pltpu.CompilerParams(
            dimension_semantics=("arbitrary", "arbitrary"),
            vmem_limit_bytes=vmem),
        name=name,
    )(*([x] * FFN_X_PIECES), wg, wu, wd, g, b, *side_casts, *head)


def _proj_kernel(repeats, x_ref, w_ref, *rest):
    n_side = len(repeats)
    side_in = rest[:n_side]
    o_ref = rest[n_side]
    side_out = rest[n_side + 1:]
    step = pl.program_id(0) * pl.num_programs(1) + pl.program_id(1)

    o_ref[...] = jnp.dot(x_ref[...], w_ref[...], preferred_element_type=F32).astype(o_ref.dtype)

    for src, dst, rep in zip(side_in, side_out, repeats):
        if rep == 1:
            dst[...] = src[...].astype(BF16)
        else:
            @pl.when(step % rep == 0)
            def _(src=src, dst=dst):
                dst[...] = src[...].astype(BF16)


def _in_proj(xb, w, side_casts):
    s, d = xb.shape
    n = w.shape[1]
    tm, tn = PROJ_TM, PROJ_TN
    gi, gj = s // tm, n // tn
    steps = gi * gj

    def step_map(rep):
        return lambda i, j: ((i * gj + j) // rep, 0)

    side_specs, side_shapes, repeats = [], [], []
    for a in side_casts:
        n_blocks = steps
        while a.shape[0] % (BF16_TILE_ROWS * n_blocks):
            n_blocks //= 2
        assert n_blocks >= 1 and steps % n_blocks == 0
        repeats.append(steps // n_blocks)
        side_specs.append(pl.BlockSpec((a.shape[0] // n_blocks, a.shape[1]),
                                       step_map(repeats[-1])))
        side_shapes.append(jax.ShapeDtypeStruct(a.shape, BF16))
    return pl.pallas_call(
        functools.partial(_proj_kernel, tuple(repeats)),
        grid=(gi, gj),
        in_specs=[
            pl.BlockSpec((tm, d), lambda i, j: (i, 0)),
            pl.BlockSpec((d, tn), lambda i, j: (0, j)),
        ] + side_specs,
        out_specs=[pl.BlockSpec((tm, tn), lambda i, j: (i, j))] + side_specs,
        out_shape=[jax.ShapeDtypeStruct((s, n), BF16)] + side_shapes,
        compiler_params=pltpu.CompilerParams(
            dimension_semantics=("arbitrary", "arbitrary"),
            vmem_limit_bytes=V7X_VMEM_LIMIT_BYTES),
        name="mixer_in_proj",
    )(xb, w, *side_casts)


def _attn_kernel(sink_ref, q_ref, kp_ref, kc_ref, kn_ref, vp_ref, vc_ref, vn_ref, bias_ref,
                 o_ref, kcat_ref, vcat_ref):
    hk = pl.program_id(0)
    i = pl.program_id(1)
    tq = q_ref.shape[0]
    blocks_per_tile = tq // BLOCK

    kcat_ref[0:BLOCK, :] = kp_ref[...]
    kcat_ref[BLOCK:BLOCK + tq, :] = kc_ref[...]
    kcat_ref[BLOCK + tq:, :] = kn_ref[...]
    vcat_ref[0:BLOCK, 0:HEAD_DIM] = vp_ref[...]
    vcat_ref[BLOCK:BLOCK + tq, 0:HEAD_DIM] = vc_ref[...]
    vcat_ref[BLOCK + tq:, 0:HEAD_DIM] = vn_ref[...]
    vcat_ref[:, HEAD_DIM:] = jnp.ones((tq + 2 * BLOCK, HEAD_DIM), BF16)

    qk_scale = LOG2E / math.sqrt(HEAD_DIM)

    def bias_variant(b):
        if b == 0:
            return jnp.where(i == 0, 0, 1)
        if b == blocks_per_tile - 1:
            return jnp.where(i == pl.num_programs(1) - 1, 2, 1)
        return 1

    def scores(b, g):
        q = q_ref[b * BLOCK:(b + 1) * BLOCK, g * HEAD_DIM:(g + 1) * HEAD_DIM]
        kw = kcat_ref[b * BLOCK:(b + 3) * BLOCK, :]
        return lax.dot_general(q, kw, (((1,), (1,)), ((), ())), preferred_element_type=F32)

    def finish(b, g, s):
        sink = sink_ref[hk * Q_PER_KV + g]
        s = s * qk_scale + bias_ref[0, bias_variant(b), g]
        m = jnp.maximum(jnp.max(s, axis=-1, keepdims=True), sink)
        m_b = jnp.broadcast_to(m, (BLOCK, HEAD_DIM))
        p = jnp.exp2(s - jnp.concatenate([m_b] * 3, axis=-1))
        vw = vcat_ref[b * BLOCK:(b + 3) * BLOCK, :]
        ov = jnp.dot(p.astype(BF16), vw, preferred_element_type=F32)
        denom = ov[:, HEAD_DIM:] + jnp.exp2(sink - m_b)
        o_ref[b * BLOCK:(b + 1) * BLOCK, g * HEAD_DIM:(g + 1) * HEAD_DIM] = (
            ov[:, :HEAD_DIM] / denom).astype(o_ref.dtype)

    units = [(b, g) for b in range(blocks_per_tile) for g in range(Q_PER_KV)]
    s_next = scores(*units[0])
    for u, unit in enumerate(units):
        s_cur = s_next
        if u + 1 < len(units):
            s_next = scores(*units[u + 1])
        finish(*unit, s_cur)


def _attention(h, bias, sink_log2):
    s = h.shape[0]
    tq = ATTN_TQ
    r = tq // BLOCK
    nb = s // BLOCK
    assert nb >= 2
    gw = Q_PER_KV * HEAD_DIM

    def prev_map(col):
        return lambda hk, i: (jnp.maximum(i * r - 1, 0), col + hk)

    def cur_map(col):
        return lambda hk, i: (i, col + hk)

    def next_map(col):
        return lambda hk, i: (jnp.minimum((i + 1) * r, nb - 1), col + hk)

    return pl.pallas_call(
        _attn_kernel,
        grid=(N_KV_HEADS, s // tq),
        in_specs=[
            pl.BlockSpec(memory_space=pltpu.SMEM),
            pl.BlockSpec((tq, gw), lambda hk, i: (i, hk)),
            pl.BlockSpec((BLOCK, HEAD_DIM), prev_map(K_COL)),
            pl.BlockSpec((tq, HEAD_DIM), cur_map(K_COL)),
            pl.BlockSpec((BLOCK, HEAD_DIM), next_map(K_COL)),
            pl.BlockSpec((BLOCK, HEAD_DIM), prev_map(V_COL)),
            pl.BlockSpec((tq, HEAD_DIM), cur_map(V_COL)),
            pl.BlockSpec((BLOCK, HEAD_DIM), next_map(V_COL)),
            pl.BlockSpec((1, 3, Q_PER_KV, BLOCK, 3 * BLOCK), lambda hk, i: (hk, 0, 0, 0, 0)),
        ],
        out_specs=pl.BlockSpec((tq, gw), lambda hk, i: (i, hk)),
        out_shape=jax.ShapeDtypeStruct((s, ATTN_WIDTH), BF16),
        scratch_shapes=[
            pltpu.VMEM((tq + 2 * BLOCK, HEAD_DIM), BF16),
            pltpu.VMEM((tq + 2 * BLOCK, 2 * HEAD_DIM), BF16),
        ],
        compiler_params=pltpu.CompilerParams(
            dimension_semantics=("parallel", "parallel"),
            vmem_limit_bytes=V7X_VMEM_LIMIT_BYTES),
        name="banded_attention",
    )(sink_log2, h, h, h, h, h, h, h, bias)


def _attn_bias_table():
    a = np.arange(BLOCK)[:, None]
    c = np.arange(3 * BLOCK)[None, :]
    dist = np.abs(a + BLOCK - c)
    slopes = np.exp2(-8.0 * np.arange(1, N_Q_HEADS + 1, dtype=np.float32) / N_Q_HEADS)
    bias = (-slopes[:, None, None] * dist.astype(np.float32)[None]
            * np.float32(LOG2E))
    in_window = dist <= WINDOW
    variants = np.stack([in_window & (c >= BLOCK), in_window, in_window & (c < 2 * BLOCK)])
    table = np.where(variants[None], bias[:, None], np.float32(NEG_INF))
    table = table.reshape(N_KV_HEADS, Q_PER_KV, 3, BLOCK, 3 * BLOCK)
    return np.ascontiguousarray(table.transpose(0, 2, 1, 3, 4), dtype=np.float32)


def _pool_band_matrices(tm):
    t = np.arange(tm)[:, None]
    col = np.arange(tm + 2 * POOL_HALO)[None, :]
    pos = np.where(col < tm + POOL_HALO, col, col - tm - 2 * POOL_HALO)
    mats = []
    for w in POOL_WINDOWS:
        lo = t - w // 2
        hi = t + w - w // 2
        mats.append(((pos >= lo) & (pos < hi)).astype(np.float32))
    return np.stack(mats)


def _mixer_out_kernel(o_ref, pp_ref, pc_ref, pn_ref, ga_ref, gb_ref, x_ref, band_ref, wgrp_ref,
                      pscale_ref, wpa_ref, wpp_ref, wout_ref, g_ref, b_ref,
                      out_ref, pcat_ref, zpre_ref):
    i = pl.program_id(0)
    n_tiles = pl.num_programs(0) - 1
    tm = pc_ref.shape[0]
    seq = n_tiles * tm

    def norm_previous_tile():
        out_ref[...] = _layer_norm(zpre_ref[...], g_ref[...], b_ref[...])

    def mix_tile(with_norm):
        zeros_halo = jnp.zeros((POOL_HALO, POOL_WIDTH), BF16)
        pcat_ref[0:tm, :] = pc_ref[...]
        pcat_ref[tm:tm + POOL_HALO, :] = jnp.where(i == n_tiles - 1, zeros_halo, pn_ref[...])
        pcat_ref[tm + POOL_HALO:, :] = jnp.where(i == 0, zeros_halo, pp_ref[...])

        if with_norm:
            norm_previous_tile()
        y_a = jnp.dot(o_ref[...], wpa_ref[...], preferred_element_type=F32)

        t = i * tm + lax.broadcasted_iota(jnp.int32, (tm, 1), 0)
        groups = range(N_POOL_GROUPS)
        cols = [slice(g * POOL_GROUP_DIM, (g + 1) * POOL_GROUP_DIM) for g in groups]
        win_sums = [jnp.dot(band_ref[g], pcat_ref[:, cols[g]], preferred_element_type=F32)
                    for g in groups]
        diffs = []
        for g, w in enumerate(POOL_WINDOWS):
            lo = jnp.clip(t - w // 2, 0, seq)
            hi = jnp.clip(t + w - w // 2, 0, seq)
            mean = win_sums[g] / (hi - lo).astype(F32)
            diffs.append((mean - pc_ref[:, cols[g]].astype(F32)).astype(BF16))
        maps = [jnp.dot(diffs[g], wgrp_ref[g], preferred_element_type=F32) for g in groups]
        mixed = jnp.concatenate(
            [(maps[g] * pscale_ref[:, cols[g]]).astype(BF16) for g in groups], axis=-1)

        y_b = jnp.dot(mixed, wpp_ref[...], preferred_element_type=F32)
        merged = (_sigmoid(ga_ref[...].astype(F32)) * y_a
                  + _sigmoid(gb_ref[...].astype(F32)) * y_b)
        z = jnp.dot(merged.astype(BF16), wout_ref[...], preferred_element_type=F32)
        zpre_ref[...] = ALPHA * x_ref[...] + z

    pl.when(i == 0)(lambda: mix_tile(False))
    pl.when((i > 0) & (i < n_tiles))(lambda: mix_tile(True))
    pl.when(i == n_tiles)(norm_previous_tile)


def _mixer_out(o, h, x, band, wgrp, pscale, wpa, wpp, wout, g, b):
    s, d = x.shape
    tm = MIX_TM
    hr = tm // POOL_HALO
    n_halo = s // POOL_HALO
    pool_cb = POOL_OFF // POOL_WIDTH
    ga_cb = GATE_A_OFF // D_MODEL
    gb_cb = GATE_B_OFF // D_MODEL
    assert POOL_OFF % POOL_WIDTH == 0 and GATE_A_OFF % D_MODEL == 0 and GATE_B_OFF % D_MODEL == 0

    n = s // tm

    def const(shape):
        return pl.BlockSpec(shape, lambda i: (0,) * len(shape), pipeline_mode=pl.Buffered(1))

    def tile(i):
        return jnp.minimum(i, n - 1)

    return pl.pallas_call(
        _mixer_out_kernel,
        grid=(n + 1,),
        in_specs=[
            pl.BlockSpec((tm, ATTN_WIDTH), lambda i: (tile(i), 0)),
            pl.BlockSpec((POOL_HALO, POOL_WIDTH),
                         lambda i: (jnp.maximum(tile(i) * hr - 1, 0), pool_cb)),
            pl.BlockSpec((tm, POOL_WIDTH), lambda i: (tile(i), pool_cb)),
            pl.BlockSpec((POOL_HALO, POOL_WIDTH),
                         lambda i: (jnp.minimum((tile(i) + 1) * hr, n_halo - 1), pool_cb)),
            pl.BlockSpec((tm, D_MODEL), lambda i: (tile(i), ga_cb)),
            pl.BlockSpec((tm, D_MODEL), lambda i: (tile(i), gb_cb)),
            pl.BlockSpec((tm, d), lambda i: (tile(i), 0)),
            const(band.shape),
            const(wgrp.shape),
            const(pscale.shape),
            const(wpa.shape),
            const(wpp.shape),
            const(wout.shape),
            const(g.shape),
            const(b.shape),
        ],
        out_specs=pl.BlockSpec((tm, d), lambda i: (jnp.maximum(i - 1, 0), 0)),
        out_shape=jax.ShapeDtypeStruct((s, d), F32),
        scratch_shapes=[pltpu.VMEM((tm + 2 * POOL_HALO, POOL_WIDTH), BF16),
                        pltpu.VMEM((tm, d), F32)],
        compiler_params=pltpu.CompilerParams(
            dimension_semantics=("arbitrary",),
            vmem_limit_bytes=V7X_VMEM_LIMIT_BYTES),
        name="mixer_out_ln",
    )(o, h, h, h, h, h, x, band, wgrp, pscale, wpa, wpp, wout, g, b)


def kernel(x, ffn1_w_gate, ffn1_w_up, ffn1_w_down, ln1_g, ln1_b, w_in, attn_sink, pool_w_groups,
           pool_scale, w_proj_attn, w_proj_pool, w_out, ln2_g, ln2_b, ffn2_w_gate, ffn2_w_up,
           ffn2_w_down, ln3_g, ln3_b):
    batch, seq, d = x.shape
    assert (batch, seq, d) == (1, SEQ, D_MODEL)
    x = x.reshape(seq, d)

    bias = jnp.asarray(_attn_bias_table())
    band = jnp.asarray(_pool_band_matrices(MIX_TM), BF16)
    depth = ffn1_w_gate.shape[0]
    for l in range(depth):
        y_head, yb_head, wg1, wu1, wd1 = _ffn_head(
            x, ffn1_w_gate[l], ffn1_w_up[l], ffn1_w_down[l], ln1_g[l][None], ln1_b[l][None])
        x, xb, w_in_b = _ffn_ln(
            x, wg1, wu1, wd1, ln1_g[l][None], ln1_b[l][None], "ffn1_ln1", True,
            side_casts=(w_in[l],), head=(y_head, yb_head))
        h, wg2, wu2, wd2, wpa, wpp, wout = _in_proj(
            xb, w_in_b, side_casts=(ffn2_w_gate[l], ffn2_w_up[l], ffn2_w_down[l],
                                    w_proj_attn[l], w_proj_pool[l], w_out[l]))
        o = _attention(h, bias, attn_sink[l].astype(F32) * LOG2E)
        x = _mixer_out(o, h, x, band, pool_w_groups[l].astype(BF16), pool_scale[l][None],
                       wpa, wpp, wout, ln2_g[l][None], ln2_b[l][None])
        (x,) = _ffn_ln(x, wg2, wu2, wd2, ln3_g[l][None], ln3_b[l][None], "ffn2_ln3", False,
                       tm=1024, slab=128, parts=4, vmem=V7X_VMEM_LIMIT_LARGE_TILE_BYTES)
    return x.reshape(batch, seq, d)
```

```python
import functools
import math

import numpy as np
import jax
import jax.numpy as jnp
from jax import lax
from jax.experimental import pallas as pl
from jax.experimental.pallas import tpu as pltpu

D_MODEL = 2048
SEQ = 16384
HEAD_DIM = 128
N_Q_HEADS = 16
N_KV_HEADS = 4
Q_PER_KV = N_Q_HEADS // N_KV_HEADS
ATTN_WIDTH = N_Q_HEADS * HEAD_DIM
KV_WIDTH = N_KV_HEADS * HEAD_DIM
WINDOW = 128
BLOCK = 128
POOL_WINDOWS = (2, 4, 8, 16)
N_POOL_GROUPS = len(POOL_WINDOWS)
POOL_GROUP_DIM = 256
POOL_WIDTH = N_POOL_GROUPS * POOL_GROUP_DIM
ALPHA = 2.0 ** 0.25
LN_EPS = 1e-5
NEG_INF = -1e30
LOG2E = math.log2(math.e)

K_COL = ATTN_WIDTH // HEAD_DIM
V_COL = (ATTN_WIDTH + KV_WIDTH) // HEAD_DIM
POOL_OFF = ATTN_WIDTH + 2 * KV_WIDTH
GATE_A_OFF = POOL_OFF + POOL_WIDTH
GATE_B_OFF = GATE_A_OFF + D_MODEL

V7X_VMEM_LIMIT_BYTES = 56 * 1024 * 1024
V7X_VMEM_LIMIT_LARGE_TILE_BYTES = 60 * 1024 * 1024
BF16_TILE_ROWS = 16
POOL_HALO = BF16_TILE_ROWS

FFN_TM = 512
FFN_TF = 512
FFN_HEAD_TF = 256
FFN_NORM_SLAB = 64
FFN_X_PIECES = 2
FFN_X_SWITCH = (3, 7)
PROJ_TM = 2048
PROJ_TN = 1024
ATTN_TQ = 2048
MIX_TM = 256

F32 = jnp.float32
BF16 = jnp.bfloat16


def _layer_norm(y, g, b):
    mu = jnp.mean(y, axis=-1, keepdims=True)
    yc = y - mu
    var = jnp.mean(yc * yc, axis=-1, keepdims=True)
    return yc * lax.rsqrt(var + LN_EPS) * g + b


def _sigmoid(x):
    return 1.0 / (1.0 + jnp.exp(-x))


def _ffn_head_kernel(x_ref, wg_ref, wu_ref, wd_ref, g_ref, b_ref,
                     y_ref, yb_ref, wgb_ref, wub_ref, wdb_ref, xb_ref, acc_ref):
    j = pl.program_id(0)

    @pl.when(j == 0)
    def _():
        x = x_ref[...]
        xb_ref[...] = x.astype(BF16)
        acc_ref[...] = (2.0 * ALPHA) * x

    wg, wu, wd = (r[...].astype(BF16) for r in (wg_ref, wu_ref, wd_ref))
    wgb_ref[...] = wg
    wub_ref[...] = wu
    wdb_ref[...] = wd
    xb = xb_ref[...]
    gate = jnp.dot(xb, wg, preferred_element_type=F32)
    up = jnp.dot(xb, wu, preferred_element_type=F32)
    act = (gate * _sigmoid(gate) * up).astype(BF16)
    acc_ref[...] += jnp.dot(act, wd, preferred_element_type=F32)

    @pl.when(j == pl.num_programs(0) - 1)
    def _():
        y = _layer_norm(0.5 * acc_ref[...], g_ref[...], b_ref[...])
        y_ref[...] = y
        yb_ref[...] = y.astype(BF16)


def _ffn_head(x, wg, wu, wd, g, b):
    s, d = x.shape
    f = wg.shape[1]
    tm, tf = FFN_TM, FFN_HEAD_TF
    nj = f // tf
    assert nj * tf == f
    return pl.pallas_call(
        _ffn_head_kernel,
        grid=(nj,),
        in_specs=[
            pl.BlockSpec((tm, d), lambda j: (0, 0)),
            pl.BlockSpec((d, tf), lambda j: (0, j)),
            pl.BlockSpec((d, tf), lambda j: (0, j)),
            pl.BlockSpec((tf, d), lambda j: (j, 0)),
            pl.BlockSpec((1, d), lambda j: (0, 0)),
            pl.BlockSpec((1, d), lambda j: (0, 0)),
        ],
        out_specs=[
            pl.BlockSpec((tm, d), lambda j: (0, 0)),
            pl.BlockSpec((tm, d), lambda j: (0, 0)),
            pl.BlockSpec((d, tf), lambda j: (0, j)),
            pl.BlockSpec((d, tf), lambda j: (0, j)),
            pl.BlockSpec((tf, d), lambda j: (j, 0)),
        ],
        out_shape=[
            jax.ShapeDtypeStruct((s, d), F32),
            jax.ShapeDtypeStruct((s, d), BF16),
            jax.ShapeDtypeStruct(wg.shape, BF16),
            jax.ShapeDtypeStruct(wu.shape, BF16),
            jax.ShapeDtypeStruct(wd.shape, BF16),
        ],
        scratch_shapes=[pltpu.VMEM((tm, d), BF16), pltpu.VMEM((tm, d), F32)],
        compiler_params=pltpu.CompilerParams(
            dimension_semantics=("arbitrary",),
            vmem_limit_bytes=V7X_VMEM_LIMIT_BYTES),
        name="ffn1_head",
    )(x, wg, wu, wd, g, b)


def _ffn_ln_kernel(n_out, n_side, slab, out_parts, *refs):
    x_refs = refs[:FFN_X_PIECES]
    wg_ref, wu_ref, wd_ref, g_ref, b_ref = refs[FFN_X_PIECES:FFN_X_PIECES + 5]
    rest = refs[FFN_X_PIECES + 5:]
    side_in = rest[:n_side]
    rest = rest[n_side:]
    xb_ref, acc_ref, ypre_ref = rest[-3:]
    rest = rest[:-3]
    out_refs = rest[-(n_out + n_side):-n_side] if n_side else rest[-n_out:]
    side_out = rest[len(rest) - n_side:]
    i = pl.program_id(0)
    j = pl.program_id(1)
    n_tiles = pl.num_programs(0) - 1
    last_j = pl.num_programs(1) - 1
    norm_steps = xb_ref.shape[0] // slab
    slabs_per_part = norm_steps // out_parts

    def down_proj(xb):
        gate = jnp.dot(xb, wg_ref[...], preferred_element_type=F32)
        up = jnp.dot(xb, wu_ref[...], preferred_element_type=F32)
        act = (gate * _sigmoid(gate) * up).astype(BF16)
        return jnp.dot(act, wd_ref[...], preferred_element_type=F32)

    def norm_slab():
        rows = pl.ds(pl.multiple_of(j * slab, slab), slab)
        part_rows = pl.ds(pl.multiple_of((j % slabs_per_part) * slab, slab), slab)
        y = _layer_norm(ypre_ref[rows, :], g_ref[...], b_ref[...])
        out_refs[0][part_rows, :] = y
        if n_out == 2:
            out_refs[1][part_rows, :] = y.astype(BF16)

    def first_chunk(with_norm):
        if with_norm:
            norm_slab()
        x = jnp.concatenate([r[...] for r in x_refs], axis=0)
        xb = x.astype(BF16)
        xb_ref[...] = xb
        acc_ref[...] = (2.0 * ALPHA) * x + down_proj(xb)

    def middle_chunk(with_norm):
        if with_norm:
            norm_slab()
        acc_ref[...] += down_proj(xb_ref[...])

    real_tile = i < n_tiles
    norm_due = (i > 0) & (j < norm_steps)
    middle = (j > 0) & (j < last_j)
    pl.when((j == 0) & (i == 0))(lambda: first_chunk(False))
    pl.when((j == 0) & (i > 0) & real_tile)(lambda: first_chunk(True))
    pl.when(middle & real_tile & norm_due)(lambda: middle_chunk(True))
    pl.when(middle & real_tile & jnp.logical_not(norm_due))(lambda: middle_chunk(False))
    pl.when(jnp.logical_not(real_tile) & norm_due)(norm_slab)

    def cast_side_blocks():
        for src, dst in zip(side_in, side_out):
            dst[...] = src[...].astype(BF16)

    @pl.when((j == last_j) & real_tile)
    def _():
        ypre_ref[...] = 0.5 * (acc_ref[...] + down_proj(xb_ref[...]))
        cast_side_blocks()

    if n_side:
        pl.when((j == last_j) & jnp.logical_not(real_tile))(cast_side_blocks)


def _ffn_ln(x, wg, wu, wd, g, b, name, with_bf16_copy, side_casts=(), head=None,
            tm=FFN_TM, slab=FFN_NORM_SLAB, parts=2, vmem=V7X_VMEM_LIMIT_BYTES):
    s, d = x.shape
    f = wg.shape[1]
    tf = FFN_TF
    first = 0 if head is None else 1
    n, nj = s // tm - first, f // tf
    n_out = 2 if with_bf16_copy else 1
    head = () if head is None else tuple(head)
    assert len(head) in (0, n_out)
    norm_steps = tm // slab
    assert norm_steps < nj and norm_steps % parts == 0 and slab % BF16_TILE_ROWS == 0

    def wcol(i, j):
        return jnp.where(i < n, j, nj - 1)

    side_specs, side_shapes = [], []
    for w in side_casts:
        n_blocks = n + first
        rows = w.shape[0] // n_blocks
        assert rows * n_blocks == w.shape[0] and rows % BF16_TILE_ROWS == 0
        side_specs.append(pl.BlockSpec((rows, w.shape[1]),
                                       lambda i, j: (jnp.minimum(i, n_blocks - 1), 0)))
        side_shapes.append(jax.ShapeDtypeStruct(w.shape, BF16))
    piece = tm // FFN_X_PIECES
    assert piece * FFN_X_PIECES == tm and len(FFN_X_SWITCH) == FFN_X_PIECES
    assert 0 < min(FFN_X_SWITCH) and max(FFN_X_SWITCH) < nj

    def x_spec(p):
        def index_map(i, j):
            tile = jnp.minimum(i + (j >= FFN_X_SWITCH[p]).astype(jnp.int32), n - 1) + first
            return (tile * FFN_X_PIECES + p, 0)
        return pl.BlockSpec((piece, d), index_map)

    steps_per_part = norm_steps // parts

    def out_index(i, j):
        done = sum((j >= (p + 1) * steps_per_part).astype(jnp.int32) for p in range(parts))
        part = jnp.where(i == 0, 0, jnp.minimum(parts * (i - 1) + done, parts * n - 1))
        return (part + parts * first, 0)

    out_spec = pl.BlockSpec((tm // parts, d), out_index)
    n_in = FFN_X_PIECES + 5 + len(side_casts)
    return pl.pallas_call(
        functools.partial(_ffn_ln_kernel, n_out, len(side_casts), slab, parts),
        grid=(n + 1, nj),
        in_specs=[x_spec(p) for p in range(FFN_X_PIECES)] + [
            pl.BlockSpec((d, tf), lambda i, j: (0, wcol(i, j))),
            pl.BlockSpec((d, tf), lambda i, j: (0, wcol(i, j))),
            pl.BlockSpec((tf, d), lambda i, j: (wcol(i, j), 0)),
            pl.BlockSpec((1, d), lambda i, j: (0, 0)),
            pl.BlockSpec((1, d), lambda i, j: (0, 0)),
        ] + side_specs + [pl.BlockSpec(memory_space=pl.ANY)] * len(head),
        input_output_aliases={n_in + k: k for k in range(len(head))},
        out_specs=[out_spec] * n_out + side_specs,
        out_shape=[jax.ShapeDtypeStruct((s, d), F32),
                   jax.ShapeDtypeStruct((s, d), BF16)][:n_out] + side_shapes,
        scratch_shapes=[pltpu.VMEM((tm, d), BF16), pltpu.VMEM((tm, d), F32),
                        pltpu.VMEM((tm, d), F32)],
        compiler_params=pltpu.CompilerParams(
            dimension_semantics=("arbitrary", "arbitrary"),
            vmem_limit_bytes=vmem),
        name=name,
    )(*([x] * FFN_X_PIECES), wg, wu, wd, g, b, *side_casts, *head)


def _proj_kernel(repeats, x_ref, w_ref, *rest):
    n_side = len(repeats)
    side_in = rest[:n_side]
    o_ref = rest[n_side]
    side_out = rest[n_side + 1:]
    step = pl.program_id(0) * pl.num_programs(1) + pl.program_id(1)

    o_ref[...] = jnp.dot(x_ref[...], w_ref[...], preferred_element_type=F32).astype(o_ref.dtype)

    for src, dst, rep in zip(side_in, side_out, repeats):
        if rep == 1:
            dst[...] = src[...].astype(BF16)
        else:
            @pl.when(step % rep == 0)
            def _(src=src, dst=dst):
                dst[...] = src[...].astype(BF16)


def _in_proj(xb, w, side_casts):
    s, d = xb.shape
    n = w.shape[1]
    tm, tn = PROJ_TM, PROJ_TN
    gi, gj = s // tm, n // tn
    steps = gi * gj

    def step_map(rep):
        return lambda i, j: ((i * gj + j) // rep, 0)

    side_specs, side_shapes, repeats = [], [], []
    for a in side_casts:
        n_blocks = steps
        while a.shape[0] % (BF16_TILE_ROWS * n_blocks):
            n_blocks //= 2
        assert n_blocks >= 1 and steps % n_blocks == 0
        repeats.append(steps // n_blocks)
        side_specs.append(pl.BlockSpec((a.shape[0] // n_blocks, a.shape[1]),
                                       step_map(repeats[-1])))
        side_shapes.append(jax.ShapeDtypeStruct(a.shape, BF16))
    return pl.pallas_call(
        functools.partial(_proj_kernel, tuple(repeats)),
        grid=(gi, gj),
        in_specs=[
            pl.BlockSpec((tm, d), lambda i, j: (i, 0)),
            pl.BlockSpec((d, tn), lambda i, j: (0, j)),
        ] + side_specs,
        out_specs=[pl.BlockSpec((tm, tn), lambda i, j: (i, j))] + side_specs,
        out_shape=[jax.ShapeDtypeStruct((s, n), BF16)] + side_shapes,
        compiler_params=pltpu.CompilerParams(
            dimension_semantics=("arbitrary", "arbitrary"),
            vmem_limit_bytes=V7X_VMEM_LIMIT_BYTES),
        name="mixer_in_proj",
    )(xb, w, *side_casts)


def _attn_kernel(sink_ref, q_ref, kp_ref, kc_ref, kn_ref, vp_ref, vc_ref, vn_ref, bias_ref,
                 o_ref, kcat_ref, vcat_ref):
    hk = pl.program_id(0)
    i = pl.program_id(1)
    tq = q_ref.shape[0]
    blocks_per_tile = tq // BLOCK

    kcat_ref[0:BLOCK, :] = kp_ref[...]
    kcat_ref[BLOCK:BLOCK + tq, :] = kc_ref[...]
    kcat_ref[BLOCK + tq:, :] = kn_ref[...]
    vcat_ref[0:BLOCK, 0:HEAD_DIM] = vp_ref[...]
    vcat_ref[BLOCK:BLOCK + tq, 0:HEAD_DIM] = vc_ref[...]
    vcat_ref[BLOCK + tq:, 0:HEAD_DIM] = vn_ref[...]
    vcat_ref[:, HEAD_DIM:] = jnp.ones((tq + 2 * BLOCK, HEAD_DIM), BF16)

    qk_scale = LOG2E / math.sqrt(HEAD_DIM)

    def bias_variant(b):
        if b == 0:
            return jnp.where(i == 0, 0, 1)
        if b == blocks_per_tile - 1:
            return jnp.where(i == pl.num_programs(1) - 1, 2, 1)
        return 1

    def scores(b, g):
        q = q_ref[b * BLOCK:(b + 1) * BLOCK, g * HEAD_DIM:(g + 1) * HEAD_DIM]
        kw = kcat_ref[b * BLOCK:(b + 3) * BLOCK, :]
        return lax.dot_general(q, kw, (((1,), (1,)), ((), ())), preferred_element_type=F32)

    def finish(b, g, s):
        sink = sink_ref[hk * Q_PER_KV + g]
        s = s * qk_scale + bias_ref[0, bias_variant(b), g]
        m = jnp.maximum(jnp.max(s, axis=-1, keepdims=True), sink)
        m_b = jnp.broadcast_to(m, (BLOCK, HEAD_DIM))
        p = jnp.exp2(s - jnp.concatenate([m_b] * 3, axis=-1))
        vw = vcat_ref[b * BLOCK:(b + 3) * BLOCK, :]
        ov = jnp.dot(p.astype(BF16), vw, preferred_element_type=F32)
        denom = ov[:, HEAD_DIM:] + jnp.exp2(sink - m_b)
        o_ref[b * BLOCK:(b + 1) * BLOCK, g * HEAD_DIM:(g + 1) * HEAD_DIM] = (
            ov[:, :HEAD_DIM] / denom).astype(o_ref.dtype)

    units = [(b, g) for b in range(blocks_per_tile) for g in range(Q_PER_KV)]
    s_next = scores(*units[0])
    for u, unit in enumerate(units):
        s_cur = s_next
        if u + 1 < len(units):
            s_next = scores(*units[u + 1])
        finish(*unit, s_cur)


def _attention(h, bias, sink_log2):
    s = h.shape[0]
    tq = ATTN_TQ
    r = tq // BLOCK
    nb = s // BLOCK
    assert nb >= 2
    gw = Q_PER_KV * HEAD_DIM

    def prev_map(col):
        return lambda hk, i: (jnp.maximum(i * r - 1, 0), col + hk)

    def cur_map(col):
        return lambda hk, i: (i, col + hk)

    def next_map(col):
        return lambda hk, i: (jnp.minimum((i + 1) * r, nb - 1), col + hk)

    return pl.pallas_call(
        _attn_kernel,
        grid=(N_KV_HEADS, s // tq),
        in_specs=[
            pl.BlockSpec(memory_space=pltpu.SMEM),
            pl.BlockSpec((tq, gw), lambda hk, i: (i, hk)),
            pl.BlockSpec((BLOCK, HEAD_DIM), prev_map(K_COL)),
            pl.BlockSpec((tq, HEAD_DIM), cur_map(K_COL)),
            pl.BlockSpec((BLOCK, HEAD_DIM), next_map(K_COL)),
            pl.BlockSpec((BLOCK, HEAD_DIM), prev_map(V_COL)),
            pl.BlockSpec((tq, HEAD_DIM), cur_map(V_COL)),
            pl.BlockSpec((BLOCK, HEAD_DIM), next_map(V_COL)),
            pl.BlockSpec((1, 3, Q_PER_KV, BLOCK, 3 * BLOCK), lambda hk, i: (hk, 0, 0, 0, 0)),
        ],
        out_specs=pl.BlockSpec((tq, gw), lambda hk, i: (i, hk)),
        out_shape=jax.ShapeDtypeStruct((s, ATTN_WIDTH), BF16),
        scratch_shapes=[
            pltpu.VMEM((tq + 2 * BLOCK, HEAD_DIM), BF16),
            pltpu.VMEM((tq + 2 * BLOCK, 2 * HEAD_DIM), BF16),
        ],
        compiler_params=pltpu.CompilerParams(
            dimension_semantics=("parallel", "parallel"),
            vmem_limit_bytes=V7X_VMEM_LIMIT_BYTES),
        name="banded_attention",
    )(sink_log2, h, h, h, h, h, h, h, bias)


def _attn_bias_table():
    a = np.arange(BLOCK)[:, None]
    c = np.arange(3 * BLOCK)[None, :]
    dist = np.abs(a + BLOCK - c)
    slopes = np.exp2(-8.0 * np.arange(1, N_Q_HEADS + 1, dtype=np.float32) / N_Q_HEADS)
    bias = (-slopes[:, None, None] * dist.astype(np.float32)[None]
            * np.float32(LOG2E))
    in_window = dist <= WINDOW
    variants = np.stack([in_window & (c >= BLOCK), in_window, in_window & (c < 2 * BLOCK)])
    table = np.where(variants[None], bias[:, None], np.float32(NEG_INF))
    table = table.reshape(N_KV_HEADS, Q_PER_KV, 3, BLOCK, 3 * BLOCK)
    return np.ascontiguousarray(table.transpose(0, 2, 1, 3, 4), dtype=np.float32)


def _pool_band_matrices(tm):
    t = np.arange(tm)[:, None]
    col = np.arange(tm + 2 * POOL_HALO)[None, :]
    pos = np.where(col < tm + POOL_HALO, col, col - tm - 2 * POOL_HALO)
    mats = []
    for w in POOL_WINDOWS:
        lo = t - w // 2
        hi = t + w - w // 2
        mats.append(((pos >= lo) & (pos < hi)).astype(np.float32))
    return np.stack(mats)


def _mixer_out_kernel(o_ref, pp_ref, pc_ref, pn_ref, ga_ref, gb_ref, x_ref, band_ref, wgrp_ref,
                      pscale_ref, wpa_ref, wpp_ref, wout_ref, g_ref, b_ref,
                      out_ref, pcat_ref, zpre_ref):
    i = pl.program_id(0)
    n_tiles = pl.num_programs(0) - 1
    tm = pc_ref.shape[0]
    seq = n_tiles * tm

    def norm_previous_tile():
        out_ref[...] = _layer_norm(zpre_ref[...], g_ref[...], b_ref[...])

    def mix_tile(with_norm):
        zeros_halo = jnp.zeros((POOL_HALO, POOL_WIDTH), BF16)
        pcat_ref[0:tm, :] = pc_ref[...]
        pcat_ref[tm:tm + POOL_HALO, :] = jnp.where(i == n_tiles - 1, zeros_halo, pn_ref[...])
        pcat_ref[tm + POOL_HALO:, :] = jnp.where(i == 0, zeros_halo, pp_ref[...])

        if with_norm:
            norm_previous_tile()
        y_a = jnp.dot(o_ref[...], wpa_ref[...], preferred_element_type=F32)

        t = i * tm + lax.broadcasted_iota(jnp.int32, (tm, 1), 0)
        groups = range(N_POOL_GROUPS)
        cols = [slice(g * POOL_GROUP_DIM, (g + 1) * POOL_GROUP_DIM) for g in groups]
        win_sums = [jnp.dot(band_ref[g], pcat_ref[:, cols[g]], preferred_element_type=F32)
                    for g in groups]
        diffs = []
        for g, w in enumerate(POOL_WINDOWS):
            lo = jnp.clip(t - w // 2, 0, seq)
            hi = jnp.clip(t + w - w // 2, 0, seq)
            mean = win_sums[g] / (hi - lo).astype(F32)
            diffs.append((mean - pc_ref[:, cols[g]].astype(F32)).astype(BF16))
        maps = [jnp.dot(diffs[g], wgrp_ref[g], preferred_element_type=F32) for g in groups]
        mixed = jnp.concatenate(
            [(maps[g] * pscale_ref[:, cols[g]]).astype(BF16) for g in groups], axis=-1)

        y_b = jnp.dot(mixed, wpp_ref[...], preferred_element_type=F32)
        merged = (_sigmoid(ga_ref[...].astype(F32)) * y_a
                  + _sigmoid(gb_ref[...].astype(F32)) * y_b)
        z = jnp.dot(merged.astype(BF16), wout_ref[...], preferred_element_type=F32)
        zpre_ref[...] = ALPHA * x_ref[...] + z

    pl.when(i == 0)(lambda: mix_tile(False))
    pl.when((i > 0) & (i < n_tiles))(lambda: mix_tile(True))
    pl.when(i == n_tiles)(norm_previous_tile)


def _mixer_out(o, h, x, band, wgrp, pscale, wpa, wpp, wout, g, b):
    s, d = x.shape
    tm = MIX_TM
    hr = tm // POOL_HALO
    n_halo = s // POOL_HALO
    pool_cb = POOL_OFF // POOL_WIDTH
    ga_cb = GATE_A_OFF // D_MODEL
    gb_cb = GATE_B_OFF // D_MODEL
    assert POOL_OFF % POOL_WIDTH == 0 and GATE_A_OFF % D_MODEL == 0 and GATE_B_OFF % D_MODEL == 0

    n = s // tm

    def const(shape):
        return pl.BlockSpec(shape, lambda i: (0,) * len(shape), pipeline_mode=pl.Buffered(1))

    def tile(i):
        return jnp.minimum(i, n - 1)

    return pl.pallas_call(
        _mixer_out_kernel,
        grid=(n + 1,),
        in_specs=[
            pl.BlockSpec((tm, ATTN_WIDTH), lambda i: (tile(i), 0)),
            pl.BlockSpec((POOL_HALO, POOL_WIDTH),
                         lambda i: (jnp.maximum(tile(i) * hr - 1, 0), pool_cb)),
            pl.BlockSpec((tm, POOL_WIDTH), lambda i: (tile(i), pool_cb)),
            pl.BlockSpec((POOL_HALO, POOL_WIDTH),
                         lambda i: (jnp.minimum((tile(i) + 1) * hr, n_halo - 1), pool_cb)),
            pl.BlockSpec((tm, D_MODEL), lambda i: (tile(i), ga_cb)),
            pl.BlockSpec((tm, D_MODEL), lambda i: (tile(i), gb_cb)),
            pl.BlockSpec((tm, d), lambda i: (tile(i), 0)),
            const(band.shape),
            const(wgrp.shape),
            const(pscale.shape),
            const(wpa.shape),
            const(wpp.shape),
            const(wout.shape),
            const(g.shape),
            const(b.shape),
        ],
        out_specs=pl.BlockSpec((tm, d), lambda i: (jnp.maximum(i - 1, 0), 0)),
        out_shape=jax.ShapeDtypeStruct((s, d), F32),
        scratch_shapes=[pltpu.VMEM((tm + 2 * POOL_HALO, POOL_WIDTH), BF16),
                        pltpu.VMEM((tm, d), F32)],
        compiler_params=pltpu.CompilerParams(
            dimension_semantics=("arbitrary",),
            vmem_limit_bytes=V7X_VMEM_LIMIT_BYTES),
        name="mixer_out_ln",
    )(o, h, h, h, h, h, x, band, wgrp, pscale, wpa, wpp, wout, g, b)


def kernel(x, ffn1_w_gate, ffn1_w_up, ffn1_w_down, ln1_g, ln1_b, w_in, attn_sink, pool_w_groups,
           pool_scale, w_proj_attn, w_proj_pool, w_out, ln2_g, ln2_b, ffn2_w_gate, ffn2_w_up,
           ffn2_w_down, ln3_g, ln3_b):
    batch, seq, d = x.shape
    assert (batch, seq, d) == (1, SEQ, D_MODEL)
    x = x.reshape(seq, d)

    bias = jnp.asarray(_attn_bias_table())
    band = jnp.asarray(_pool_band_matrices(MIX_TM), BF16)
    depth = ffn1_w_gate.shape[0]
    for l in range(depth):
        y_head, yb_head, wg1, wu1, wd1 = _ffn_head(
            x, ffn1_w_gate[l], ffn1_w_up[l], ffn1_w_down[l], ln1_g[l][None], ln1_b[l][None])
        x, xb, w_in_b = _ffn_ln(
            x, wg1, wu1, wd1, ln1_g[l][None], ln1_b[l][None], "ffn1_ln1", True,
            side_casts=(w_in[l],), head=(y_head, yb_head))
        h, wg2, wu2, wd2, wpa, wpp, wout = _in_proj(
            xb, w_in_b, side_casts=(ffn2_w_gate[l], ffn2_w_up[l], ffn2_w_down[l],
                                    w_proj_attn[l], w_proj_pool[l], w_out[l]))
        o = _attention(h, bias, attn_sink[l].astype(F32) * LOG2E)
        x = _mixer_out(o, h, x, band, pool_w_groups[l].astype(BF16), pool_scale[l][None],
                       wpa, wpp, wout, ln2_g[l][None], ln2_b[l][None])
        (x,) = _ffn_ln(x, wg2, wu2, wd2, ln3_g[l][None], ln3_b[l][None], "ffn2_ln3", False,
                       tm=1024, slab=128, parts=4, vmem=V7X_VMEM_LIMIT_LARGE_TILE_BYTES)
    return x.reshape(batch, seq, d)
```
